```python
import math
import jax, jax.numpy as jnp
from jax import lax
import numpy as np

D_MODEL = 1024
BATCH = 32
SEQ = 2048
DEPTH = 1

D_FF = 2816
SSM_EXPAND = 2
SSM_D_INNER = SSM_EXPAND * D_MODEL
SSM_HEAD_DIM = 64
SSM_HEADS = SSM_D_INNER // SSM_HEAD_DIM
SSM_GROUPS = 8
SSM_HEADS_PER_GROUP = SSM_HEADS // SSM_GROUPS
SSM_STATE = 128
SSM_CONV = 4
SSM_CHUNK = 128
SSM_CONV_DIM = SSM_D_INNER + 2 * SSM_GROUPS * SSM_STATE
ATT_HEADS = 8
ATT_HEAD_DIM = 128
ATT_WIDTH = ATT_HEADS * ATT_HEAD_DIM
MOBA_BLOCK = 256
MOBA_TOPK = 3
Q_BLOCK = 128
N_BRANCHES = 2
IN_COLS = SSM_D_INNER + SSM_CONV_DIM + SSM_HEADS + 3 * ATT_WIDTH + N_BRANCHES * D_MODEL
DEEPNORM_ALPHA = (2 * DEPTH) ** 0.25
DEEPNORM_BETA = (8 * DEPTH) ** -0.25
LN_EPS = 1e-5
RMS_EPS = 1e-5

kernel_name = "hybrid_ssd_moba_macaron_deepnorm"


def layer_norm(x, g, b):
    xf = x.astype(jnp.float32)
    mu = xf.mean(-1, keepdims=True)
    var = jnp.square(xf - mu).mean(-1, keepdims=True)
    y = (xf - mu) * lax.rsqrt(var + LN_EPS) * g.astype(jnp.float32) + b.astype(jnp.float32)
    return y.astype(x.dtype)


def swiglu(x, w_gate, w_up, w_down):
    return (jax.nn.silu(x @ w_gate) * (x @ w_up)) @ w_down


def causal_depthwise_conv(u, w, b):
    out = lax.conv_general_dilated(
        u, w[:, None, :].astype(u.dtype), window_strides=(1,), padding=[(SSM_CONV - 1, 0)],
        dimension_numbers=("NWC", "WIO", "NWC"), feature_group_count=u.shape[-1])
    return out + b


def ssd_chunked_scan(xh, dt, a, bm, cm):
    bsz, s, g, r, p = xh.shape
    n = bm.shape[-1]
    nc = s // SSM_CHUNK

    def to_chunks(t):
        return jnp.moveaxis(t.reshape(bsz, nc, SSM_CHUNK, *t.shape[2:]), 1, 0)

    xs = (to_chunks(xh * dt[..., None]), to_chunks(dt * a), to_chunks(bm), to_chunks(cm))
    causal = jnp.tril(jnp.ones((SSM_CHUNK, SSM_CHUNK), dtype=bool))

    def step(state, inp):
        xdt, da, b, c = inp
        acum = jnp.moveaxis(jnp.cumsum(da, axis=1), 1, -1)
        seg = acum[..., :, None] - acum[..., None, :]
        decay_ls = jnp.exp(jnp.where(causal, seg, -jnp.inf))
        cb = jnp.einsum("blgn,bsgn->bgls", c, b)
        y = jnp.einsum("bgrls,bsgrp->blgrp", decay_ls * cb[:, :, None], xdt)
        y = y + jnp.einsum("blgn,bgrpn,bgrl->blgrp", c, state, jnp.exp(acum))
        last = acum[..., -1:]
        new_state = state * jnp.exp(last)[..., None] + jnp.einsum(
            "bsgn,bgrs,bsgrp->bgrpn", b, jnp.exp(last - acum), xdt)
        return new_state, y

    state0 = jnp.zeros((bsz, g, r, p, n), jnp.float32)
    _, ys = lax.scan(step, state0, xs)
    return jnp.moveaxis(ys, 0, 1).reshape(bsz, s, g, r, p)


def mamba2_branch(z, xbc, dt_raw, conv_w, conv_b, dt_bias, a_log, d_skip, norm_w):
    bsz, s, _ = z.shape
    g, r, p, n = SSM_GROUPS, SSM_HEADS_PER_GROUP, SSM_HEAD_DIM, SSM_STATE
    xbc = jax.nn.silu(causal_depthwise_conv(xbc, conv_w, conv_b))
    xs, bm, cm = jnp.split(xbc, [SSM_D_INNER, SSM_D_INNER + g * n], axis=-1)
    xh = xs.reshape(bsz, s, g, r, p).astype(jnp.float32)
    bm = bm.reshape(bsz, s, g, n).astype(jnp.float32)
    cm = cm.reshape(bsz, s, g, n).astype(jnp.float32)
    dt = jax.nn.softplus(dt_raw.astype(jnp.float32) + dt_bias.astype(jnp.float32)).reshape(bsz, s, g, r)
    a = -jnp.exp(a_log.astype(jnp.float32)).reshape(g, r)
    y = ssd_chunked_scan(xh, dt, a, bm, cm)
    y = y + d_skip.astype(jnp.float32).reshape(g, r)[:, :, None] * xh
    y = y.reshape(bsz, s, SSM_D_INNER) * jax.nn.silu(z.astype(jnp.float32))
    yg = y.reshape(bsz, s, SSM_GROUPS, -1)
    yg = yg * lax.rsqrt(jnp.mean(jnp.square(yg), -1, keepdims=True) + RMS_EPS)
    return (yg.reshape(bsz, s, SSM_D_INNER) * norm_w.astype(jnp.float32)).astype(z.dtype)


def moba_attention(q, k, v):
    bsz, s, h, dh = q.shape
    nb = -(-s // MOBA_BLOCK)
    s_pad = nb * MOBA_BLOCK
    nqb = s // Q_BLOCK
    topk = min(MOBA_TOPK, nb)
    scale = 1.0 / math.sqrt(dh)
    q = q.transpose(0, 2, 1, 3)
    k = k.transpose(0, 2, 1, 3)
    v = v.transpose(0, 2, 1, 3)
    pad = ((0, 0), (0, 0), (0, s_pad - s), (0, 0))
    k_blocks = jnp.pad(k, pad).reshape(bsz, h, nb, MOBA_BLOCK, dh)
    v_blocks = jnp.pad(v, pad).reshape(bsz, h, nb, MOBA_BLOCK, dh)
    k_mean = k_blocks.astype(jnp.float32).mean(axis=3)
    gate = jnp.einsum("bhsd,bhjd->bhsj", q.astype(jnp.float32), k_mean)
    q_blk = jnp.arange(s) // MOBA_BLOCK
    fully_past = jnp.arange(nb)[None, :] < q_blk[:, None]
    gate = jnp.where(fully_past, gate, -jnp.inf)
    _, sel = lax.top_k(gate, topk)
    valid = sel < q_blk[:, None]

    def to_qblocks(t):
        c = t.shape[-1]
        return t.reshape(bsz, h, nqb, Q_BLOCK, c).transpose(0, 2, 1, 3, 4).reshape(bsz * nqb, h, Q_BLOCK, c)

    b_idx = jnp.repeat(jnp.arange(bsz), nqb)
    qb_idx = jnp.tile(jnp.arange(nqb), bsz)
    head_idx = jnp.arange(h)[:, None, None]
    offs_q = jnp.arange(Q_BLOCK)
    offs_k = jnp.arange(MOBA_BLOCK)

    def one_block(inp):
        b, qi, qb, sb, ok = inp
        kb = k_blocks[b]
        vb = v_blocks[b]
        own = (qi * Q_BLOCK) // MOBA_BLOCK
        k_own = lax.dynamic_index_in_dim(kb, own, axis=1, keepdims=False)
        v_own = lax.dynamic_index_in_dim(vb, own, axis=1, keepdims=False)
        k_sel = kb[head_idx, sb]
        v_sel = vb[head_idx, sb]
        s_own = jnp.einsum("hqd,hkd->hqk", qb, k_own).astype(jnp.float32) * scale
        causal = (own * MOBA_BLOCK + offs_k)[None, :] <= (qi * Q_BLOCK + offs_q)[:, None]
        s_own = jnp.where(causal, s_own, -jnp.inf)
        s_sel = jnp.einsum("hqd,hqjkd->hqjk", qb, k_sel).astype(jnp.float32) * scale
        s_sel = jnp.where(ok[..., None], s_sel, -jnp.inf)
        scores = jnp.concatenate([s_own, s_sel.reshape(h, Q_BLOCK, topk * MOBA_BLOCK)], axis=-1)
        probs = jax.nn.softmax(scores, axis=-1).astype(v.dtype)
        p_own = probs[..., :MOBA_BLOCK]
        p_sel = probs[..., MOBA_BLOCK:].reshape(h, Q_BLOCK, topk, MOBA_BLOCK)
        return (jnp.einsum("hqk,hkd->hqd", p_own, v_own)
                + jnp.einsum("hqjk,hqjkd->hqd", p_sel, v_sel))

    out = lax.map(one_block, (b_idx, qb_idx, to_qblocks(q), to_qblocks(sel), to_qblocks(valid)))
    return out.reshape(bsz, nqb, h, Q_BLOCK, dh).transpose(0, 1, 3, 2, 4).reshape(bsz, s, h * dh)


def hybrid_mixer(h, w_in, conv_w, conv_b, dt_bias, a_log, d_skip, ssm_norm_w,
                 w_ssm_out, w_att_out, b_gate, w_o):
    bsz, s, _ = h.shape
    proj = h @ w_in
    splits = np.cumsum([SSM_D_INNER, SSM_CONV_DIM, SSM_HEADS, ATT_WIDTH, ATT_WIDTH, ATT_WIDTH]).tolist()
    z, xbc, dt_raw, q, k, v, g_raw = jnp.split(proj, splits, axis=-1)
    y_ssm = mamba2_branch(z, xbc, dt_raw, conv_w, conv_b, dt_bias, a_log, d_skip, ssm_norm_w) @ w_ssm_out
    shp = (bsz, s, ATT_HEADS, ATT_HEAD_DIM)
    y_att = moba_attention(q.reshape(shp), k.reshape(shp), v.reshape(shp)) @ w_att_out
    gates = jax.nn.sigmoid((g_raw + b_gate).astype(jnp.float32)).astype(h.dtype)
    g_ssm, g_att = jnp.split(gates, N_BRANCHES, axis=-1)
    return (g_ssm * y_ssm + g_att * y_att) @ w_o


def setup_inputs(seed: int = 0) -> dict:
    key = jax.random.key(seed)
    ks = jax.random.split(key, 32)
    f32 = jnp.float32

    def nrm(k, shape, scale):
        return jax.random.normal(k, shape, f32) * scale

    def gain(k, n):
        return 1.0 + 0.01 * jax.random.normal(k, (n,), f32)

    dt0 = jnp.exp(jax.random.uniform(ks[10], (SSM_HEADS,), f32)
                  * (math.log(0.1) - math.log(0.001)) + math.log(0.001))
    dt0 = jnp.maximum(dt0, 1e-4)
    return {
        "x": jax.random.normal(ks[0], (BATCH, SEQ, D_MODEL), f32),
        "ffn1_w_gate": nrm(ks[1], (D_MODEL, D_FF), D_MODEL ** -0.5),
        "ffn1_w_up": nrm(ks[2], (D_MODEL, D_FF), D_MODEL ** -0.5),
        "ffn1_w_down": nrm(ks[3], (D_FF, D_MODEL), DEEPNORM_BETA * D_FF ** -0.5),
        "ln1_g": gain(ks[4], D_MODEL),
        "ln1_b": nrm(ks[5], (D_MODEL,), 0.01),
        "w_in": nrm(ks[6], (D_MODEL, IN_COLS), D_MODEL ** -0.5),
        "conv_w": nrm(ks[7], (SSM_CONV, SSM_CONV_DIM), SSM_CONV ** -0.5),
        "conv_b": nrm(ks[8], (SSM_CONV_DIM,), 0.01),
        "dt_bias": dt0 + jnp.log(-jnp.expm1(-dt0)),
        "a_log": jnp.log(jax.random.uniform(ks[11], (SSM_HEADS,), f32, 1.0, 16.0)),
        "d_skip": gain(ks[12], SSM_HEADS),
        "ssm_norm_w": gain(ks[13], SSM_D_INNER),
        "w_ssm_out": nrm(ks[14], (SSM_D_INNER, D_MODEL), SSM_D_INNER ** -0.5),
        "w_att_out": nrm(ks[15], (ATT_WIDTH, D_MODEL), ATT_WIDTH ** -0.5),
        "b_gate": nrm(ks[16], (N_BRANCHES * D_MODEL,), 0.01),
        "w_o": nrm(ks[17], (D_MODEL, D_MODEL), DEEPNORM_BETA * D_MODEL ** -0.5),
        "ln2_g": gain(ks[18], D_MODEL),
        "ln2_b": nrm(ks[19], (D_MODEL,), 0.01),
        "ffn2_w_gate": nrm(ks[20], (D_MODEL, D_FF), D_MODEL ** -0.5),
        "ffn2_w_up": nrm(ks[21], (D_MODEL, D_FF), D_MODEL ** -0.5),
        "ffn2_w_down": nrm(ks[22], (D_FF, D_MODEL), DEEPNORM_BETA * D_FF ** -0.5),
        "ln3_g": gain(ks[23], D_MODEL),
        "ln3_b": nrm(ks[24], (D_MODEL,), 0.01),
    }


def reference(x, ffn1_w_gate, ffn1_w_up, ffn1_w_down, ln1_g, ln1_b, w_in, conv_w, conv_b,
              dt_bias, a_log, d_skip, ssm_norm_w, w_ssm_out, w_att_out, b_gate, w_o,
              ln2_g, ln2_b, ffn2_w_gate, ffn2_w_up, ffn2_w_down, ln3_g, ln3_b):
    h = x
    for _ in range(DEPTH):
        h = layer_norm(DEEPNORM_ALPHA * h + 0.5 * swiglu(h, ffn1_w_gate, ffn1_w_up, ffn1_w_down), ln1_g, ln1_b)
        h = layer_norm(DEEPNORM_ALPHA * h + hybrid_mixer(h, w_in, conv_w, conv_b, dt_bias, a_log, d_skip,
                                                         ssm_norm_w, w_ssm_out, w_att_out, b_gate, w_o),
                       ln2_g, ln2_b)
        h = layer_norm(DEEPNORM_ALPHA * h + 0.5 * swiglu(h, ffn2_w_gate, ffn2_w_up, ffn2_w_down), ln3_g, ln3_b)
    return h
```

```python
import functools
import math

import jax
import jax.numpy as jnp
from jax import lax
from jax.experimental import pallas as pl
from jax.experimental.pallas import tpu as pltpu

F32 = jnp.float32
BF16 = jnp.bfloat16

SSM_HEAD_DIM = 64
SSM_GROUPS = 8
SSM_STATE = 128
SSM_CONV = 4
SSM_CHUNK = 128
ATT_HEAD_DIM = 128
MOBA_BLOCK = 256
MOBA_TOPK = 3
DEPTH = 1
DEEPNORM_ALPHA = (2 * DEPTH) ** 0.25
LN_EPS = 1e-5
RMS_EPS = 1e-5

V7X_VMEM_BYTES = 64 * 1024 * 1024
V7X_SUBLANES = 8
VMEM_LIMIT_BYTES = V7X_VMEM_BYTES - 8 * 1024 * 1024

FFN_ROWS = 512
FFN_CHUNK = 256
PROJ_ROWS = 1024
PROJ_COLS = 1024
VT_ROWS = 512

NT_DIMS = (((1,), (1,)), ((), ()))


def _params(semantics):
    return pltpu.CompilerParams(dimension_semantics=semantics, vmem_limit_bytes=VMEM_LIMIT_BYTES)


def _resident(shape):
    nd = len(shape)
    return pl.BlockSpec(shape, lambda *_: (0,) * nd, pipeline_mode=pl.Buffered(1))


def _dot(a, b):
    return jnp.dot(a, b, preferred_element_type=F32)


def _dot_nt(a, b):
    return lax.dot_general(a, b, NT_DIMS, preferred_element_type=F32)


def _silu(v):
    return v * jax.nn.sigmoid(v)


def _layer_norm(r, g, b):
    mu = jnp.mean(r, axis=-1, keepdims=True)
    d = r - mu
    var = jnp.mean(d * d, axis=-1, keepdims=True)
    return d * lax.rsqrt(var + LN_EPS) * g + b


def _ffn_ln_kernel(x_ref, wg_ref, wu_ref, wd_ref, g_ref, b_ref, *refs, d_ff, with_bf16_out):
    if with_bf16_out:
        o_ref, ob_ref, act_ref = refs
    else:
        o_ref, act_ref = refs
    x = x_ref[...]
    xb = x.astype(BF16)
    for c in range(d_ff // FFN_CHUNK):
        sl = slice(c * FFN_CHUNK, (c + 1) * FFN_CHUNK)
        gate = _dot(xb, wg_ref[:, sl])
        up = _dot(xb, wu_ref[:, sl])
        act_ref[:, sl] = (_silu(gate) * up).astype(BF16)
    y = _dot(act_ref[...], wd_ref[...])
    out = _layer_norm(DEEPNORM_ALPHA * x + 0.5 * y, g_ref[...], b_ref[...])
    o_ref[...] = out
    if with_bf16_out:
        ob_ref[...] = out.astype(BF16)


def _ffn_ln(x, wg, wu, wd, g, b, *, with_bf16_out):
    n, d = x.shape
    d_ff = wg.shape[1]
    assert n % FFN_ROWS == 0 and d_ff % FFN_CHUNK == 0
    row = lambda i: (i, 0)
    out_shape = [jax.ShapeDtypeStruct((n, d), F32)]
    out_specs = [pl.BlockSpec((FFN_ROWS, d), row)]
    if with_bf16_out:
        out_shape.append(jax.ShapeDtypeStruct((n, d), BF16))
        out_specs.append(pl.BlockSpec((FFN_ROWS, d), row))
    return pl.pallas_call(
        functools.partial(_ffn_ln_kernel, d_ff=d_ff, with_bf16_out=with_bf16_out),
        grid=(n // FFN_ROWS,),
        in_specs=[pl.BlockSpec((FFN_ROWS, d), row),
                  _resident((d, d_ff)), _resident((d, d_ff)), _resident((d_ff, d)),
                  _resident((1, d)), _resident((1, d))],
        out_specs=out_specs,
        out_shape=out_shape,
        scratch_shapes=[pltpu.VMEM((FFN_ROWS, d_ff), BF16)],
        compiler_params=_params(("parallel",)),
        name="ffn_ln",
    )(x, wg.astype(BF16), wu.astype(BF16), wd.astype(BF16), g.reshape(1, d), b.reshape(1, d))


def _proj_kernel(x_ref, w_ref, o_ref):
    o_ref[...] = _dot(x_ref[...], w_ref[...]).astype(o_ref.dtype)


def _proj(xb, w):
    n, d = xb.shape
    cols = w.shape[1]
    assert n % PROJ_ROWS == 0 and cols % PROJ_COLS == 0
    return pl.pallas_call(
        _proj_kernel,
        grid=(n // PROJ_ROWS, cols // PROJ_COLS),
        in_specs=[pl.BlockSpec((PROJ_ROWS, d), lambda i, j: (i, 0)),
                  pl.BlockSpec((d, PROJ_COLS), lambda i, j: (0, j))],
        out_specs=pl.BlockSpec((PROJ_ROWS, PROJ_COLS), lambda i, j: (i, j)),
        out_shape=jax.ShapeDtypeStruct((n, cols), BF16),
        compiler_params=_params(("parallel", "arbitrary")),
        name="in_proj",
    )(xb, w)


def _proj_t_kernel(x_ref, wv_ref, wdt_ref, vt_ref, dtt_ref):
    x = x_ref[...]
    vt_ref[0] = _dot_nt(wv_ref[...], x).astype(BF16)
    dtt_ref[0] = _dot_nt(wdt_ref[...], x)


def _proj_t(xb, w_vt, w_dtt, bsz, seq):
    n, d = xb.shape
    cv, cdt = w_vt.shape[0], w_dtt.shape[0]
    assert seq % VT_ROWS == 0
    spb = seq // VT_ROWS
    return pl.pallas_call(
        _proj_t_kernel,
        grid=(bsz, spb),
        in_specs=[pl.BlockSpec((VT_ROWS, d), lambda b, s: (b * spb + s, 0)),
                  _resident((cv, d)), _resident((cdt, d))],
        out_specs=[pl.BlockSpec((1, cv, VT_ROWS), lambda b, s: (b, 0, s)),
                   pl.BlockSpec((1, cdt, VT_ROWS), lambda b, s: (b, 0, s))],
        out_shape=[jax.ShapeDtypeStruct((bsz, cv, seq), BF16),
                   jax.ShapeDtypeStruct((bsz, cdt, seq), F32)],
        compiler_params=_params(("parallel", "arbitrary")),
        name="proj_t",
    )(xb, w_vt, w_dtt)


def _cumsum_lanes(v):
    n = v.shape[-1]
    lane = lax.broadcasted_iota(jnp.int32, v.shape, v.ndim - 1)
    shift = 1
    while shift < n:
        v = v + jnp.where(lane >= shift, pltpu.roll(v, shift, v.ndim - 1), 0.0)
        shift *= 2
    return v


def _softplus(v):
    return jnp.maximum(v, 0.0) + jnp.log1p(jnp.exp(-jnp.abs(v)))


def _ssd_kernel(xbc_ref, z_ref, dt_ref, convw_ref, convb_ref, dtb_ref, alog_ref, dskip_ref,
                normw_ref, o_ref, state_ref, ubuf_ref, *, d_inner, n_heads):
    L, P, N, G = SSM_CHUNK, SSM_HEAD_DIM, SSM_STATE, SSM_GROUPS
    hpg = n_heads // G
    gw = hpg * P
    pad = V7X_SUBLANES

    @pl.when(pl.program_id(1) == 0)
    def _():
        state_ref[...] = jnp.zeros_like(state_ref)
        ubuf_ref[0:pad, :] = jnp.zeros((pad, ubuf_ref.shape[1]), F32)

    dt = _softplus(dt_ref[0] + dtb_ref[...])
    a = -jnp.exp(alog_ref[...])
    acum = _cumsum_lanes(dt * a)
    last = acum[:, L - 1:L]
    wrow = dt * jnp.exp(last - acum)
    elast = jnp.exp(last)
    acum_t = acum.T
    causal = (lax.broadcasted_iota(jnp.int32, (L, L), 0)
              >= lax.broadcasted_iota(jnp.int32, (L, L), 1))

    ubuf_ref[pad:pad + L, :] = xbc_ref[...].astype(F32)

    def conv_silu(lo, width):
        acc = convb_ref[:, lo:lo + width]
        for k in range(SSM_CONV):
            r0 = pad - (SSM_CONV - 1) + k
            acc = acc + convw_ref[k:k + 1, lo:lo + width] * ubuf_ref[r0:r0 + L, lo:lo + width]
        return _silu(acc)

    for g in range(G):
        x_g = conv_silu(g * gw, gw)
        b_g = conv_silu(d_inner + g * N, N).astype(BF16)
        c_g = conv_silu(d_inner + G * N + g * N, N)
        xb_g = x_g.astype(BF16)
        cb = _dot_nt(c_g.astype(BF16), b_g)
        x_t = x_g.T
        ys = []
        for r in range(hpg):
            h = g * hpg + r
            acol = acum_t[:, h:h + 1]
            decay = jnp.exp(jnp.where(causal, acol - acum[h:h + 1, :], -jnp.inf))
            m = decay * cb * dt[h:h + 1, :]
            st = state_ref[h]
            c_e = (c_g * jnp.exp(acol)).astype(BF16)
            ys.append(_dot(m.astype(BF16), xb_g[:, r * P:(r + 1) * P])
                      + _dot_nt(c_e, st.astype(BF16)))
            xw = (x_t[r * P:(r + 1) * P, :] * wrow[h:h + 1, :]).astype(BF16)
            state_ref[h] = st * elast[h:h + 1, :] + _dot(xw, b_g)
        sl = slice(g * gw, (g + 1) * gw)
        y = jnp.concatenate(ys, axis=1) + dskip_ref[:, sl] * x_g
        y = y * _silu(z_ref[:, sl].astype(F32))
        y = y * lax.rsqrt(jnp.mean(y * y, axis=-1, keepdims=True) + RMS_EPS)
        o_ref[:, sl] = (y * normw_ref[:, sl]).astype(BF16)

    ubuf_ref[0:pad, :] = ubuf_ref[L:L + pad, :]


def _ssd(proj, dt_t, conv_w, conv_b, dt_bias, a_log, d_skip, norm_w, *, bsz, seq, d_inner, n_heads):
    conv_dim = conv_w.shape[1]
    L = SSM_CHUNK
    assert seq % L == 0 and conv_dim % d_inner == 0
    nc = seq // L
    z_blk = conv_dim // d_inner
    tok = lambda b, c: (b * nc + c, 0)
    return pl.pallas_call(
        functools.partial(_ssd_kernel, d_inner=d_inner, n_heads=n_heads),
        grid=(bsz, nc),
        in_specs=[pl.BlockSpec((L, conv_dim), tok),
                  pl.BlockSpec((L, d_inner), lambda b, c: (b * nc + c, z_blk)),
                  pl.BlockSpec((1, n_heads, L), lambda b, c: (b, 0, c)),
                  _resident((SSM_CONV, conv_dim)), _resident((1, conv_dim)),
                  _resident((n_heads, 1)), _resident((n_heads, 1)),
                  _resident((1, d_inner)), _resident((1, d_inner))],
        out_specs=pl.BlockSpec((L, d_inner), tok),
        out_shape=jax.ShapeDtypeStruct((bsz * seq, d_inner), BF16),
        scratch_shapes=[pltpu.VMEM((n_heads, SSM_HEAD_DIM, SSM_STATE), F32),
                        pltpu.VMEM((L + 2 * V7X_SUBLANES, conv_dim), F32)],
        compiler_params=_params(("parallel", "arbitrary")),
        name="ssd",
    )(proj, proj, dt_t, conv_w, conv_b.reshape(1, conv_dim),
      dt_bias.reshape(n_heads, 1), a_log.reshape(n_heads, 1),
      jnp.repeat(d_skip, SSM_HEAD_DIM).reshape(1, d_inner), norm_w.reshape(1, d_inner))


def _split3_bf16(v):
    hi = v.astype(BF16)
    r1 = v - hi.astype(F32)
    mid = r1.astype(BF16)
    lo = (r1 - mid.astype(F32)).astype(BF16)
    return hi, mid, lo


def _moba_kernel(q_ref, k_ref, vt_ref, o_ref, *, nb):
    blk = MOBA_BLOCK
    seq = nb * blk
    scale = 1.0 / math.sqrt(ATT_HEAD_DIM)

    bi = lax.broadcasted_iota(jnp.int32, (nb, seq), 0)
    ti = lax.broadcasted_iota(jnp.int32, (nb, seq), 1)
    in_blk = (ti >= bi * blk) & (ti < (bi + 1) * blk)
    avg = jnp.where(in_blk, 1.0 / blk, 0.0).astype(BF16)
    k_mean = _dot(avg, k_ref[...])
    q_all = q_ref[...]
    gate = sum(_dot_nt(part, q_all) for part in _split3_bf16(k_mean))

    key_idx = lax.broadcasted_iota(jnp.int32, (blk, blk), 0)
    qry_idx = lax.broadcasted_iota(jnp.int32, (blk, blk), 1)
    causal = key_idx <= qry_idx

    for i in range(nb):
        qs = slice(i * blk, (i + 1) * blk)
        q_i = q_ref[qs, :]

        s = _dot_nt(k_ref[qs, :], q_i) * scale
        s = jnp.where(causal, s, -jnp.inf)
        m = jnp.max(s, axis=0, keepdims=True)
        p = jnp.exp(s - m)
        l = jnp.sum(p, axis=0, keepdims=True)
        acc = _dot(vt_ref[:, qs], p.astype(BF16))

        g_rows = [gate[j:j + 1, qs] for j in range(i)]
        for j in range(i):
            ks = slice(j * blk, (j + 1) * blk)
            s = _dot_nt(k_ref[ks, :], q_i) * scale
            if i > MOBA_TOPK:
                rank = jnp.zeros((1, blk), jnp.int32)
                for jj in range(i):
                    if jj < j:
                        rank = rank + (g_rows[jj] >= g_rows[j]).astype(jnp.int32)
                    elif jj > j:
                        rank = rank + (g_rows[jj] > g_rows[j]).astype(jnp.int32)
                s = jnp.where(rank < MOBA_TOPK, s, -jnp.inf)
            m_new = jnp.maximum(m, jnp.max(s, axis=0, keepdims=True))
            alpha = jnp.exp(m - m_new)
            p = jnp.exp(s - m_new)
            l = alpha * l + jnp.sum(p, axis=0, keepdims=True)
            acc = acc * alpha + _dot(vt_ref[:, ks], p.astype(BF16))
            m = m_new

        o_ref[qs, :] = (acc / l).T.astype(BF16)


def _moba(proj, v_t, *, bsz, seq, n_heads, q_col, k_col):
    d = ATT_HEAD_DIM
    assert seq % MOBA_BLOCK == 0
    nb = seq // MOBA_BLOCK
    return pl.pallas_call(
        functools.partial(_moba_kernel, nb=nb),
        grid=(bsz, n_heads),
        in_specs=[pl.BlockSpec((seq, d), lambda b, h: (b, q_col + h)),
                  pl.BlockSpec((seq, d), lambda b, h: (b, k_col + h)),
                  pl.BlockSpec((None, d, seq), lambda b, h: (b, h, 0))],
        out_specs=pl.BlockSpec((seq, d), lambda b, h: (b, h)),
        out_shape=jax.ShapeDtypeStruct((bsz * seq, n_heads * d), BF16),
        compiler_params=_params(("parallel", "arbitrary")),
        name="moba",
    )(proj, proj, v_t)


def _merge_ln_kernel(ys_ref, ya_ref, gr_ref, h_ref, wso_ref, wao_ref, wo_ref, bg_ref,
                     lg_ref, lb_ref, o_ref):
    d = h_ref.shape[1]
    y_ssm = _dot(ys_ref[...], wso_ref[...])
    y_att = _dot(ya_ref[...], wao_ref[...])
    gates = jax.nn.sigmoid(gr_ref[...].astype(F32) + bg_ref[...])
    merged = gates[:, :d] * y_ssm + gates[:, d:] * y_att
    mix = _dot(merged.astype(BF16), wo_ref[...])
    o_ref[...] = _layer_norm(DEEPNORM_ALPHA * h_ref[...] + mix, lg_ref[...], lb_ref[...])


def _merge_ln(y_ssm, y_att, proj, h, w_ssm_out, w_att_out, w_o, b_gate, ln_g, ln_b, *, gate_col):
    n, d = h.shape
    d_inner, att_w = y_ssm.shape[1], y_att.shape[1]
    row = lambda i: (i, 0)
    return pl.pallas_call(
        _merge_ln_kernel,
        grid=(n // FFN_ROWS,),
        in_specs=[pl.BlockSpec((FFN_ROWS, d_inner), row),
                  pl.BlockSpec((FFN_ROWS, att_w), row),
                  pl.BlockSpec((FFN_ROWS, 2 * d), lambda i: (i, gate_col)),
                  pl.BlockSpec((FFN_ROWS, d), row),
                  _resident((d_inner, d)), _resident((att_w, d)), _resident((d, d)),
                  _resident((1, 2 * d)), _resident((1, d)), _resident((1, d))],
        out_specs=pl.BlockSpec((FFN_ROWS, d), row),
        out_shape=jax.ShapeDtypeStruct((n, d), F32),
        compiler_params=_params(("parallel",)),
        name="merge_ln",
    )(y_ssm, y_att, proj, h, w_ssm_out.astype(BF16), w_att_out.astype(BF16), w_o.astype(BF16),
      b_gate.reshape(1, 2 * d), ln_g.reshape(1, d), ln_b.reshape(1, d))


def kernel(x, ffn1_w_gate, ffn1_w_up, ffn1_w_down, ln1_g, ln1_b, w_in, conv_w, conv_b, dt_bias,
           a_log, d_skip, ssm_norm_w, w_ssm_out, w_att_out, b_gate, w_o, ln2_g, ln2_b,
           ffn2_w_gate, ffn2_w_up, ffn2_w_down, ln3_g, ln3_b):
    bsz, seq, d = x.shape
    n = bsz * seq
    d_inner = w_ssm_out.shape[0]
    conv_dim = conv_w.shape[1]
    n_ssm_heads = dt_bias.shape[0]
    att_w = w_att_out.shape[0]
    n_att_heads = att_w // ATT_HEAD_DIM

    edges = [0]
    for width in (d_inner, conv_dim, n_ssm_heads, att_w, att_w, att_w, 2 * d):
        edges.append(edges[-1] + width)
    assert edges[-1] == w_in.shape[1]
    w_z, w_xbc, w_dt, w_q, w_k, w_v, w_g = (w_in[:, lo:hi] for lo, hi in zip(edges[:-1], edges[1:]))
    w_main = jnp.concatenate([w_xbc, w_z, w_q, w_k, w_g], axis=1).astype(BF16)
    q_off = conv_dim + d_inner
    k_off = q_off + att_w
    g_off = k_off + att_w
    assert q_off % ATT_HEAD_DIM == 0 and g_off % (2 * d) == 0

    h1, h1b = _ffn_ln(x.reshape(n, d), ffn1_w_gate, ffn1_w_up, ffn1_w_down, ln1_g, ln1_b,
                      with_bf16_out=True)
    proj = _proj(h1b, w_main)
    v_t, dt_t = _proj_t(h1b, w_v.T.astype(BF16), w_dt.T.astype(BF16), bsz, seq)
    y_ssm = _ssd(proj, dt_t, conv_w, conv_b, dt_bias, a_log, d_skip, ssm_norm_w,
                 bsz=bsz, seq=seq, d_inner=d_inner, n_heads=n_ssm_heads)
    y_att = _moba(proj, v_t, bsz=bsz, seq=seq, n_heads=n_att_heads,
                  q_col=q_off // ATT_HEAD_DIM, k_col=k_off // ATT_HEAD_DIM)
    h2 = _merge_ln(y_ssm, y_att, proj, h1, w_ssm_out, w_att_out, w_o, b_gate, ln2_g, ln2_b,
                   gate_col=g_off // (2 * d))
    (out,) = _ffn_ln(h2, ffn2_w_gate, ffn2_w_up, ffn2_w_down, ln3_g, ln3_b, with_bf16_out=False)
    return out.reshape(bsz, seq, d)
```

```python
import functools
import math

import jax
import jax.numpy as jnp
from jax import lax
from jax.experimental import pallas as pl
from jax.experimental.pallas import tpu as pltpu

F32 = jnp.float32
BF16 = jnp.bfloat16

SSM_HEAD_DIM = 64
SSM_GROUPS = 8
SSM_STATE = 128
SSM_CONV = 4
SSM_CHUNK = 128
ATT_HEAD_DIM = 128
MOBA_BLOCK = 256
MOBA_TOPK = 3
DEPTH = 1
DEEPNORM_ALPHA = (2 * DEPTH) ** 0.25
LN_EPS = 1e-5
RMS_EPS = 1e-5

V7X_VMEM_BYTES = 64 * 1024 * 1024
V7X_SUBLANES = 8
VMEM_LIMIT_BYTES = V7X_VMEM_BYTES - 8 * 1024 * 1024

FFN_ROWS = 512
FFN_CHUNK = 256
PROJ_ROWS = 1024
PROJ_COLS = 1024
VT_ROWS = 512

NT_DIMS = (((1,), (1,)), ((), ()))
TN_DIMS = (((0,), (0,)), ((), ()))
LOG2E = 1.4426950408889634


def _params(semantics):
    return pltpu.CompilerParams(dimension_semantics=semantics, vmem_limit_bytes=VMEM_LIMIT_BYTES)


def _resident(shape):
    nd = len(shape)
    return pl.BlockSpec(shape, lambda *_: (0,) * nd, pipeline_mode=pl.Buffered(1))


def _dot(a, b):
    return jnp.dot(a, b, preferred_element_type=F32)


def _dot_nt(a, b):
    return lax.dot_general(a, b, NT_DIMS, preferred_element_type=F32)


def _silu(v):
    return v * jax.nn.sigmoid(v)


def _layer_norm(r, g, b):
    mu = jnp.mean(r, axis=-1, keepdims=True)
    d = r - mu
    var = jnp.mean(d * d, axis=-1, keepdims=True)
    return d * lax.rsqrt(var + LN_EPS) * g + b


def _ffn_ln_kernel(x_ref, wg_ref, wu_ref, wd_ref, g_ref, b_ref, *refs, d_ff, with_bf16_out):
    if with_bf16_out:
        o_ref, ob_ref, act_ref = refs
    else:
        o_ref, act_ref = refs
    x = x_ref[...]
    xb = x.astype(BF16)
    for c in range(d_ff // FFN_CHUNK):
        sl = slice(c * FFN_CHUNK, (c + 1) * FFN_CHUNK)
        gate = _dot(xb, wg_ref[:, sl])
        up = _dot(xb, wu_ref[:, sl])
        act_ref[:, sl] = (_silu(gate) * up).astype(BF16)
    y = _dot(act_ref[...], wd_ref[...])
    out = _layer_norm(DEEPNORM_ALPHA * x + 0.5 * y, g_ref[...], b_ref[...])
    o_ref[...] = out
    if with_bf16_out:
        ob_ref[...] = out.astype(BF16)


def _ffn_ln(x, wg, wu, wd, g, b, *, with_bf16_out):
    n, d = x.shape
    d_ff = wg.shape[1]
    assert n % FFN_ROWS == 0 and d_ff % FFN_CHUNK == 0
    row = lambda i: (i, 0)
    out_shape = [jax.ShapeDtypeStruct((n, d), F32)]
    out_specs = [pl.BlockSpec((FFN_ROWS, d), row)]
    if with_bf16_out:
        out_shape.append(jax.ShapeDtypeStruct((n, d), BF16))
        out_specs.append(pl.BlockSpec((FFN_ROWS, d), row))
    return pl.pallas_call(
        functools.partial(_ffn_ln_kernel, d_ff=d_ff, with_bf16_out=with_bf16_out),
        grid=(n // FFN_ROWS,),
        in_specs=[pl.BlockSpec((FFN_ROWS, d), row),
                  _resident((d, d_ff)), _resident((d, d_ff)), _resident((d_ff, d)),
                  _resident((1, d)), _resident((1, d))],
        out_specs=out_specs,
        out_shape=out_shape,
        scratch_shapes=[pltpu.VMEM((FFN_ROWS, d_ff), BF16)],
        compiler_params=_params(("parallel",)),
        name="ffn_ln",
    )(x, wg.astype(BF16), wu.astype(BF16), wd.astype(BF16), g.reshape(1, d), b.reshape(1, d))


def _proj_kernel(x_ref, w_ref, o_ref):
    o_ref[...] = _dot(x_ref[...], w_ref[...]).astype(o_ref.dtype)


def _proj(xb, w):
    n, d = xb.shape
    cols = w.shape[1]
    assert n % PROJ_ROWS == 0 and cols % PROJ_COLS == 0
    return pl.pallas_call(
        _proj_kernel,
        grid=(n // PROJ_ROWS, cols // PROJ_COLS),
        in_specs=[pl.BlockSpec((PROJ_ROWS, d), lambda i, j: (i, 0)),
                  pl.BlockSpec((d, PROJ_COLS), lambda i, j: (0, j))],
        out_specs=pl.BlockSpec((PROJ_ROWS, PROJ_COLS), lambda i, j: (i, j)),
        out_shape=jax.ShapeDtypeStruct((n, cols), BF16),
        compiler_params=_params(("parallel", "arbitrary")),
        name="in_proj",
    )(xb, w)


def _proj_t_kernel(x_ref, wv_ref, wdt_ref, vt_ref, dtt_ref):
    x = x_ref[...]
    vt_ref[0] = _dot_nt(wv_ref[...], x).astype(BF16)
    dtt_ref[0] = _dot_nt(wdt_ref[...], x)


def _proj_t(xb, w_vt, w_dtt, bsz, seq):
    n, d = xb.shape
    cv, cdt = w_vt.shape[0], w_dtt.shape[0]
    assert seq % VT_ROWS == 0
    spb = seq // VT_ROWS
    return pl.pallas_call(
        _proj_t_kernel,
        grid=(bsz, spb),
        in_specs=[pl.BlockSpec((VT_ROWS, d), lambda b, s: (b * spb + s, 0)),
                  _resident((cv, d)), _resident((cdt, d))],
        out_specs=[pl.BlockSpec((1, cv, VT_ROWS), lambda b, s: (b, 0, s)),
                   pl.BlockSpec((1, cdt, VT_ROWS), lambda b, s: (b, 0, s))],
        out_shape=[jax.ShapeDtypeStruct((bsz, cv, seq), BF16),
                   jax.ShapeDtypeStruct((bsz, cdt, seq), F32)],
        compiler_params=_params(("parallel", "arbitrary")),
        name="proj_t",
    )(xb, w_vt, w_dtt)


def _cumsum_lanes(v):
    n = v.shape[-1]
    lane = lax.broadcasted_iota(jnp.int32, v.shape, v.ndim - 1)
    shift = 1
    while shift < n:
        v = v + jnp.where(lane >= shift, pltpu.roll(v, shift, v.ndim - 1), 0.0)
        shift *= 2
    return v


def _softplus(v):
    return jnp.maximum(v, 0.0) + jnp.log1p(jnp.exp(-jnp.abs(v)))


def _split3_bf16(v):
    hi = v.astype(BF16)
    r1 = v - hi.astype(F32)
    mid = r1.astype(BF16)
    lo = (r1 - mid.astype(F32)).astype(BF16)
    return hi, mid, lo


def _split3_t(v):
    parts = [p.astype(F32) for p in _split3_bf16(v)] + [jnp.zeros_like(v)]
    return jnp.concatenate(parts, axis=0).T.astype(BF16)


def _ssd_kernel(xbc_ref, z_ref, dt_ref, convw_ref, convb_ref, dtb_ref, alog_ref, dskip_ref,
                normw_ref, shift_ref, headsel_ref, headdiag_ref, chansel_ref,
                o_ref, state_ref, ubuf_ref, x_ref, bc_ref, *, n_heads):
    L, P, N, G = SSM_CHUNK, SSM_HEAD_DIM, SSM_STATE, SSM_GROUPS
    H = n_heads
    hpg = H // G
    gw = hpg * P
    slab = gw + 2 * N
    hist = ubuf_ref.shape[0] - L

    @pl.when(pl.program_id(1) == 0)
    def _():
        state_ref[...] = jnp.zeros_like(state_ref)
        ubuf_ref[0:hist, :] = jnp.zeros((hist, ubuf_ref.shape[1]), BF16)

    ubuf_ref[hist:hist + L, :] = xbc_ref[...]
    for g in range(G):
        cs = slice(g * slab, (g + 1) * slab)
        u = ubuf_ref[:, cs]
        shifted = _dot(shift_ref[...], u)
        acc = convb_ref[:, cs] + convw_ref[SSM_CONV - 1:SSM_CONV, cs] * u[hist:].astype(F32)
        for k in range(SSM_CONV - 1):
            acc = acc + convw_ref[k:k + 1, cs] * shifted[k * L:(k + 1) * L]
        act = _silu(acc)
        x_ref[:, g * gw:(g + 1) * gw] = act[:, :gw]
        bc_ref[:, g * 2 * N:(g + 1) * 2 * N] = act[:, gw:].astype(BF16)
    ubuf_ref[0:hist, :] = ubuf_ref[L:L + hist, :]

    dt = _softplus(dt_ref[0] + dtb_ref[...])
    a = -jnp.exp(alog_ref[...])
    acum = _cumsum_lanes(dt * a)
    last = acum[:, L - 1:L]
    wrow = dt * jnp.exp(last - acum)
    acum2 = acum * LOG2E

    neg_rows = [pltpu.repeat(-p, H, axis=1) * headdiag_ref[...] for p in _split3_bf16(acum2)]
    neg_rows.append(jnp.zeros((H, H * L), BF16))
    seg = _dot(jnp.concatenate([_split3_t(acum2), jnp.ones((L, 4 * H), BF16)], axis=1),
               jnp.concatenate([headsel_ref[...]] + neg_rows, axis=0))
    chan = _dot(jnp.concatenate([_split3_t(dt), _split3_t(wrow), _split3_t(jnp.exp(acum))], axis=0),
                chansel_ref[...])
    causal = (lax.broadcasted_iota(jnp.int32, (L, L), 0)
              >= lax.broadcasted_iota(jnp.int32, (L, L), 1))
    head_of_lane = lax.broadcasted_iota(jnp.int32, (L, gw), 1) // P

    def b_of(g):
        return bc_ref[:, g * 2 * N:g * 2 * N + N]

    def c_of(g):
        return bc_ref[:, g * 2 * N + N:(g + 1) * 2 * N]

    cb_next = _dot_nt(c_of(0), b_of(0))
    for g in range(G):
        sl = slice(g * gw, (g + 1) * gw)
        cb = cb_next
        if g + 1 < G:
            cb_next = _dot_nt(c_of(g + 1), b_of(g + 1))
        x_g, b_g, c_g = x_ref[:, sl], b_of(g), c_of(g)
        st = state_ref[g]
        y_state = _dot(c_g, st.astype(BF16))
        xw = (x_g * chan[L:2 * L, sl]).astype(BF16)
        state_ref[g] = (st * chan[3 * L - 1:3 * L, sl]
                        + lax.dot_general(b_g, xw, TN_DIMS, preferred_element_type=F32))
        m_g = jnp.concatenate(
            [(jnp.exp2(jnp.where(causal, seg[:, h * L:(h + 1) * L], -jnp.inf)) * cb).astype(BF16)
             for h in range(g * hpg, (g + 1) * hpg)], axis=1)
        xdt = (x_g * chan[0:L, sl]).astype(BF16)
        xdt_diag = jnp.concatenate(
            [jnp.where(head_of_lane == r, xdt, jnp.zeros_like(xdt)) for r in range(hpg)], axis=0)
        y = _dot(m_g, xdt_diag) + y_state * chan[2 * L:3 * L, sl] + dskip_ref[:, sl] * x_g
        y = y * _silu(z_ref[:, sl].astype(F32))
        y = y * lax.rsqrt(jnp.mean(y * y, axis=-1, keepdims=True) + RMS_EPS)
        o_ref[:, sl] = (y * normw_ref[:, sl]).astype(BF16)


def _ssd_constants(n_heads, d_inner):
    L, H = SSM_CHUNK, n_heads
    hist = 2 * V7X_SUBLANES
    t = jnp.arange(L)[:, None]
    src = jnp.arange(hist + L)[None, :]
    shift = jnp.concatenate([(src == t + hist - d).astype(BF16)
                             for d in range(SSM_CONV - 1, 0, -1)], axis=0)
    k = jnp.arange(4 * H)[:, None]
    live = k < 3 * H
    headsel = (live & (k % H == jnp.arange(H * L)[None, :] // L)).astype(BF16)
    headdiag = (jnp.arange(H)[:, None] == jnp.arange(H * L)[None, :] // L).astype(BF16)
    chansel = (live & (k % H == jnp.arange(d_inner)[None, :] // SSM_HEAD_DIM)).astype(BF16)
    return shift, headsel, headdiag, chansel, hist


def _conv_group_major(t, d_inner):
    G, N = SSM_GROUPS, SSM_STATE
    gw = d_inner // G
    parts = []
    for g in range(G):
        parts += [t[..., g * gw:(g + 1) * gw],
                  t[..., d_inner + g * N:d_inner + (g + 1) * N],
                  t[..., d_inner + (G + g) * N:d_inner + (G + g + 1) * N]]
    return jnp.concatenate(parts, axis=-1)


def _ssd(proj, dt_t, conv_w, conv_b, dt_bias, a_log, d_skip, norm_w, *, bsz, seq, d_inner, n_heads):
    conv_dim = conv_w.shape[1]
    L = SSM_CHUNK
    assert seq % L == 0 and conv_dim % d_inner == 0 and 4 * n_heads == L
    assert conv_dim == d_inner + 2 * SSM_GROUPS * SSM_STATE
    nc = seq // L
    z_blk = conv_dim // d_inner
    gw = d_inner // SSM_GROUPS
    shift, headsel, headdiag, chansel, hist = _ssd_constants(n_heads, d_inner)
    tok = lambda b, c: (b * nc + c, 0)
    return pl.pallas_call(
        functools.partial(_ssd_kernel, n_heads=n_heads),
        grid=(bsz, nc),
        in_specs=[pl.BlockSpec((L, conv_dim), tok),
                  pl.BlockSpec((L, d_inner), lambda b, c: (b * nc + c, z_blk)),
                  pl.BlockSpec((1, n_heads, L), lambda b, c: (b, 0, c)),
                  _resident((SSM_CONV, conv_dim)), _resident((1, conv_dim)),
                  _resident((n_heads, 1)), _resident((n_heads, 1)),
                  _resident((1, d_inner)), _resident((1, d_inner)),
                  _resident(shift.shape), _resident(headsel.shape), _resident(headdiag.shape),
                  _resident(chansel.shape)],
        out_specs=pl.BlockSpec((L, d_inner), tok),
        out_shape=jax.ShapeDtypeStruct((bsz * seq, d_inner), BF16),
        scratch_shapes=[pltpu.VMEM((SSM_GROUPS, SSM_STATE, gw), F32),
                        pltpu.VMEM((hist + L, conv_dim), BF16),
                        pltpu.VMEM((L, d_inner), F32),
                        pltpu.VMEM((L, conv_dim - d_inner), BF16)],
        compiler_params=_params(("parallel", "arbitrary")),
        name="ssd",
    )(proj, proj, dt_t, conv_w, conv_b.reshape(1, conv_dim),
      dt_bias.reshape(n_heads, 1), a_log.reshape(n_heads, 1),
      jnp.repeat(d_skip, SSM_HEAD_DIM).reshape(1, d_inner), norm_w.reshape(1, d_inner),
      shift, headsel, headdiag, chansel)


def _moba_kernel(q_ref, k_ref, vt_ref, o_ref, *, nb):
    blk = MOBA_BLOCK
    seq = nb * blk

    bi = lax.broadcasted_iota(jnp.int32, (nb, seq), 0)
    ti = lax.broadcasted_iota(jnp.int32, (nb, seq), 1)
    in_blk = (ti >= bi * blk) & (ti < (bi + 1) * blk)
    avg = jnp.where(in_blk, 1.0 / blk, 0.0).astype(BF16)

    key_idx = lax.broadcasted_iota(jnp.int32, (blk, blk), 0)
    qry_idx = lax.broadcasted_iota(jnp.int32, (blk, blk), 1)
    causal = key_idx <= qry_idx

    def scores(i):
        return _dot_nt(k_ref[0:(i + 1) * blk, :], q_ref[i * blk:(i + 1) * blk, :])

    s_next = scores(0)
    k_mean = _dot(avg, k_ref[...])
    q_all = q_ref[...]
    gate = sum(_dot_nt(part, q_all) for part in _split3_bf16(k_mean))

    for i in range(nb):
        qs = slice(i * blk, (i + 1) * blk)
        n_keys = (i + 1) * blk
        s_all = s_next
        if i + 1 < nb:
            s_next = scores(i + 1)
        g_rows = [gate[j:j + 1, qs] for j in range(i)]

        blocks = []
        for j in range(i + 1):
            s = s_all[j * blk:(j + 1) * blk]
            if j == i:
                s = jnp.where(causal, s, -jnp.inf)
            elif i > MOBA_TOPK:
                rank = jnp.zeros((1, blk), jnp.int32)
                for jj in range(i):
                    if jj < j:
                        rank = rank + (g_rows[jj] >= g_rows[j]).astype(jnp.int32)
                    elif jj > j:
                        rank = rank + (g_rows[jj] > g_rows[j]).astype(jnp.int32)
                s = jnp.where(rank < MOBA_TOPK, s, -jnp.inf)
            blocks.append(s)

        m = functools.reduce(jnp.maximum, [jnp.max(s, axis=0, keepdims=True) for s in blocks])
        probs = [jnp.exp2(s - m) for s in blocks]
        l = sum(jnp.sum(p, axis=0, keepdims=True) for p in probs)
        p_all = jnp.concatenate([p.astype(BF16) for p in probs], axis=0)
        acc = _dot(vt_ref[:, 0:n_keys], p_all)
        o_ref[qs, :] = (acc * (1.0 / l)).T.astype(BF16)


def _moba(proj, v_t, *, bsz, seq, n_heads, q_col, k_col):
    d = ATT_HEAD_DIM
    assert seq % MOBA_BLOCK == 0
    nb = seq // MOBA_BLOCK
    return pl.pallas_call(
        functools.partial(_moba_kernel, nb=nb),
        grid=(bsz, n_heads),
        in_specs=[pl.BlockSpec((seq, d), lambda b, h: (b, q_col + h)),
                  pl.BlockSpec((seq, d), lambda b, h: (b, k_col + h)),
                  pl.BlockSpec((None, d, seq), lambda b, h: (b, h, 0))],
        out_specs=pl.BlockSpec((seq, d), lambda b, h: (b, h)),
        out_shape=jax.ShapeDtypeStruct((bsz * seq, n_heads * d), BF16),
        compiler_params=_params(("parallel", "arbitrary")),
        name="moba",
    )(proj, proj, v_t)


def _merge_ln_kernel(ys_ref, ya_ref, gr_ref, h_ref, wso_ref, wao_ref, wo_ref, bg_ref,
                     lg_ref, lb_ref, o_ref):
    d = h_ref.shape[1]
    y_ssm = _dot(ys_ref[...], wso_ref[...])
    y_att = _dot(ya_ref[...], wao_ref[...])
    gates = jax.nn.sigmoid(gr_ref[...].astype(F32) + bg_ref[...])
    merged = gates[:, :d] * y_ssm + gates[:, d:] * y_att
    mix = _dot(merged.astype(BF16), wo_ref[...])
    o_ref[...] = _layer_norm(DEEPNORM_ALPHA * h_ref[...] + mix, lg_ref[...], lb_ref[...])


def _merge_ln(y_ssm, y_att, proj, h, w_ssm_out, w_att_out, w_o, b_gate, ln_g, ln_b, *, gate_col):
    n, d = h.shape
    d_inner, att_w = y_ssm.shape[1], y_att.shape[1]
    row = lambda i: (i, 0)
    return pl.pallas_call(
        _merge_ln_kernel,
        grid=(n // FFN_ROWS,),
        in_specs=[pl.BlockSpec((FFN_ROWS, d_inner), row),
                  pl.BlockSpec((FFN_ROWS, att_w), row),
                  pl.BlockSpec((FFN_ROWS, 2 * d), lambda i: (i, gate_col)),
                  pl.BlockSpec((FFN_ROWS, d), row),
                  _resident((d_inner, d)), _resident((att_w, d)), _resident((d, d)),
                  _resident((1, 2 * d)), _resident((1, d)), _resident((1, d))],
        out_specs=pl.BlockSpec((FFN_ROWS, d), row),
        out_shape=jax.ShapeDtypeStruct((n, d), F32),
        compiler_params=_params(("parallel",)),
        name="merge_ln",
    )(y_ssm, y_att, proj, h, w_ssm_out.astype(BF16), w_att_out.astype(BF16), w_o.astype(BF16),
      b_gate.reshape(1, 2 * d), ln_g.reshape(1, d), ln_b.reshape(1, d))


def kernel(x, ffn1_w_gate, ffn1_w_up, ffn1_w_down, ln1_g, ln1_b, w_in, conv_w, conv_b, dt_bias,
           a_log, d_skip, ssm_norm_w, w_ssm_out, w_att_out, b_gate, w_o, ln2_g, ln2_b,
           ffn2_w_gate, ffn2_w_up, ffn2_w_down, ln3_g, ln3_b):
    bsz, seq, d = x.shape
    n = bsz * seq
    d_inner = w_ssm_out.shape[0]
    conv_dim = conv_w.shape[1]
    n_ssm_heads = dt_bias.shape[0]
    att_w = w_att_out.shape[0]
    n_att_heads = att_w // ATT_HEAD_DIM

    edges = [0]
    for width in (d_inner, conv_dim, n_ssm_heads, att_w, att_w, att_w, 2 * d):
        edges.append(edges[-1] + width)
    assert edges[-1] == w_in.shape[1]
    w_z, w_xbc, w_dt, w_q, w_k, w_v, w_g = (w_in[:, lo:hi] for lo, hi in zip(edges[:-1], edges[1:]))
    w_q = w_q * (LOG2E / math.sqrt(ATT_HEAD_DIM))
    w_xbc = _conv_group_major(w_xbc, d_inner)
    w_main = jnp.concatenate([w_xbc, w_z, w_q, w_k, w_g], axis=1).astype(BF16)
    q_off = conv_dim + d_inner
    k_off = q_off + att_w
    g_off = k_off + att_w
    assert q_off % ATT_HEAD_DIM == 0 and g_off % (2 * d) == 0

    h1, h1b = _ffn_ln(x.reshape(n, d), ffn1_w_gate, ffn1_w_up, ffn1_w_down, ln1_g, ln1_b,
                      with_bf16_out=True)
    proj = _proj(h1b, w_main)
    v_t, dt_t = _proj_t(h1b, w_v.T.astype(BF16), w_dt.T.astype(BF16), bsz, seq)
    y_ssm = _ssd(proj, dt_t, _conv_group_major(conv_w, d_inner), _conv_group_major(conv_b, d_inner),
                 dt_bias, a_log, d_skip, ssm_norm_w,
                 bsz=bsz, seq=seq, d_inner=d_inner, n_heads=n_ssm_heads)
    y_att = _moba(proj, v_t, bsz=bsz, seq=seq, n_heads=n_att_heads,
                  q_col=q_off // ATT_HEAD_DIM, k_col=k_off // ATT_HEAD_DIM)
    h2 = _merge_ln(y_ssm, y_att, proj, h1, w_ssm_out, w_att_out, w_o, b_gate, ln2_g, ln2_b,
                   gate_col=g_off // (2 * d))
    (out,) = _ffn_ln(h2, ffn2_w_gate, ffn2_w_up, ffn2_w_down, ln3_g, ln3_b, with_bf16_out=False)
    return out.reshape(bsz, seq, d)
```

```python
import functools
import math

import jax
import jax.numpy as jnp
from jax import lax
from jax.experimental import pallas as pl
from jax.experimental.pallas import tpu as pltpu

F32 = jnp.float32
BF16 = jnp.bfloat16

SSM_HEAD_DIM = 64
SSM_GROUPS = 8
SSM_STATE = 128
SSM_CONV = 4
SSM_CHUNK = 128
ATT_HEAD_DIM = 128
MOBA_BLOCK = 256
MOBA_TOPK = 3
DEPTH = 1
DEEPNORM_ALPHA = (2 * DEPTH) ** 0.25
LN_EPS = 1e-5
RMS_EPS = 1e-5

V7X_VMEM_BYTES = 64 * 1024 * 1024
V7X_SUBLANES = 8
VMEM_LIMIT_BYTES = V7X_VMEM_BYTES - 8 * 1024 * 1024

FFN_ROWS = 512
FFN_CHUNK = 256
PROJ_ROWS = 1024
PROJ_COLS = 1024
VT_ROWS = 512

NT_DIMS = (((1,), (1,)), ((), ()))
TN_DIMS = (((0,), (0,)), ((), ()))
LOG2E = 1.4426950408889634


def _params(semantics):
    return pltpu.CompilerParams(dimension_semantics=semantics, vmem_limit_bytes=VMEM_LIMIT_BYTES)


def _resident(shape):
    nd = len(shape)
    return pl.BlockSpec(shape, lambda *_: (0,) * nd, pipeline_mode=pl.Buffered(1))


def _dot(a, b):
    return jnp.dot(a, b, preferred_element_type=F32)


def _dot_nt(a, b):
    return lax.dot_general(a, b, NT_DIMS, preferred_element_type=F32)


def _silu(v):
    return v * jax.nn.sigmoid(v)


def _layer_norm(r, g, b):
    mu = jnp.mean(r, axis=-1, keepdims=True)
    d = r - mu
    var = jnp.mean(d * d, axis=-1, keepdims=True)
    return d * lax.rsqrt(var + LN_EPS) * g + b


def _ffn_ln_kernel(x_ref, wg_ref, wu_ref, wd_ref, g_ref, b_ref, *refs, d_ff, with_bf16_out):
    if with_bf16_out:
        o_ref, ob_ref, act_ref = refs
    else:
        o_ref, act_ref = refs
    x = x_ref[...]
    xb = x.astype(BF16)
    for c in range(d_ff // FFN_CHUNK):
        sl = slice(c * FFN_CHUNK, (c + 1) * FFN_CHUNK)
        gate = _dot(xb, wg_ref[:, sl])
        up = _dot(xb, wu_ref[:, sl])
        act_ref[:, sl] = (_silu(gate) * up).astype(BF16)
    y = _dot(act_ref[...], wd_ref[...])
    out = _layer_norm(DEEPNORM_ALPHA * x + 0.5 * y, g_ref[...], b_ref[...])
    o_ref[...] = out
    if with_bf16_out:
        ob_ref[...] = out.astype(BF16)


def _ffn_ln(x, wg, wu, wd, g, b, *, with_bf16_out):
    n, d = x.shape
    d_ff = wg.shape[1]
    assert n % FFN_ROWS == 0 and d_ff % FFN_CHUNK == 0
    row = lambda i: (i, 0)
    out_shape = [jax.ShapeDtypeStruct((n, d), F32)]
    out_specs = [pl.BlockSpec((FFN_ROWS, d), row)]
    if with_bf16_out:
        out_shape.append(jax.ShapeDtypeStruct((n, d), BF16))
        out_specs.append(pl.BlockSpec((FFN_ROWS, d), row))
    return pl.pallas_call(
        functools.partial(_ffn_ln_kernel, d_ff=d_ff, with_bf16_out=with_bf16_out),
        grid=(n // FFN_ROWS,),
        in_specs=[pl.BlockSpec((FFN_ROWS, d), row),
                  _resident((d, d_ff)), _resident((d, d_ff)), _resident((d_ff, d)),
                  _resident((1, d)), _resident((1, d))],
        out_specs=out_specs,
        out_shape=out_shape,
        scratch_shapes=[pltpu.VMEM((FFN_ROWS, d_ff), BF16)],
        compiler_params=_params(("parallel",)),
        name="ffn_ln",
    )(x, wg.astype(BF16), wu.astype(BF16), wd.astype(BF16), g.reshape(1, d), b.reshape(1, d))


def _proj_kernel(x_ref, w_ref, o_ref):
    o_ref[...] = _dot(x_ref[...], w_ref[...]).astype(o_ref.dtype)


def _proj(xb, w):
    n, d = xb.shape
    cols = w.shape[1]
    assert n % PROJ_ROWS == 0 and cols % PROJ_COLS == 0
    return pl.pallas_call(
        _proj_kernel,
        grid=(n // PROJ_ROWS, cols // PROJ_COLS),
        in_specs=[pl.BlockSpec((PROJ_ROWS, d), lambda i, j: (i, 0)),
                  pl.BlockSpec((d, PROJ_COLS), lambda i, j: (0, j))],
        out_specs=pl.BlockSpec((PROJ_ROWS, PROJ_COLS), lambda i, j: (i, j)),
        out_shape=jax.ShapeDtypeStruct((n, cols), BF16),
        compiler_params=_params(("parallel", "arbitrary")),
        name="in_proj",
    )(xb, w)


ONES_ROWS = 2 * V7X_SUBLANES


def _proj_t_kernel(x_ref, wv_ref, wdt_ref, vt_ref, dtt_ref):
    x = x_ref[...]
    n_heads, rows, cols = vt_ref.shape[1:]
    hd = rows - ONES_ROWS
    v_t = _dot_nt(wv_ref[...], x).astype(BF16)
    for h in range(n_heads):
        vt_ref[0, h, 0:hd, :] = v_t[h * hd:(h + 1) * hd]
        vt_ref[0, h, hd:rows, :] = jnp.ones((ONES_ROWS, cols), BF16)
    dtt_ref[0] = _dot_nt(wdt_ref[...], x)


def _proj_t(xb, w_vt, w_dtt, bsz, seq):
    n, d = xb.shape
    cv, cdt = w_vt.shape[0], w_dtt.shape[0]
    assert seq % VT_ROWS == 0 and cv % ATT_HEAD_DIM == 0
    spb = seq // VT_ROWS
    n_heads = cv // ATT_HEAD_DIM
    rows = ATT_HEAD_DIM + ONES_ROWS
    return pl.pallas_call(
        _proj_t_kernel,
        grid=(bsz, spb),
        in_specs=[pl.BlockSpec((VT_ROWS, d), lambda b, s: (b * spb + s, 0)),
                  _resident((cv, d)), _resident((cdt, d))],
        out_specs=[pl.BlockSpec((1, n_heads, rows, VT_ROWS), lambda b, s: (b, 0, 0, s)),
                   pl.BlockSpec((1, cdt, VT_ROWS), lambda b, s: (b, 0, s))],
        out_shape=[jax.ShapeDtypeStruct((bsz, n_heads, rows, seq), BF16),
                   jax.ShapeDtypeStruct((bsz, cdt, seq), F32)],
        compiler_params=_params(("parallel", "arbitrary")),
        name="proj_t",
    )(xb, w_vt, w_dtt)


def _cumsum_lanes(v):
    n = v.shape[-1]
    lane = lax.broadcasted_iota(jnp.int32, v.shape, v.ndim - 1)
    shift = 1
    while shift < n:
        v = v + jnp.where(lane >= shift, pltpu.roll(v, shift, v.ndim - 1), 0.0)
        shift *= 2
    return v


def _softplus(v):
    return jnp.maximum(v, 0.0) + jnp.log1p(jnp.exp(-jnp.abs(v)))


def _split3_bf16(v):
    hi = v.astype(BF16)
    r1 = v - hi.astype(F32)
    mid = r1.astype(BF16)
    lo = (r1 - mid.astype(F32)).astype(BF16)
    return hi, mid, lo


def _split3_t(v):
    parts = [p.astype(F32) for p in _split3_bf16(v)] + [jnp.zeros_like(v)]
    return jnp.concatenate(parts, axis=0).T.astype(BF16)


def _ssd_kernel(xbc_ref, z_ref, dt_ref, convw_ref, convb_ref, dtb_ref, alog_ref, dskip_ref,
                normw_ref, shift_ref, headsel_ref, headdiag_ref, chansel_ref,
                o_ref, state_ref, ubuf_ref, x_ref, bc_ref, *, n_heads):
    L, P, N, G = SSM_CHUNK, SSM_HEAD_DIM, SSM_STATE, SSM_GROUPS
    H = n_heads
    hpg = H // G
    gw = hpg * P
    slab = gw + 2 * N
    hist = ubuf_ref.shape[0] - L

    @pl.when(pl.program_id(1) == 0)
    def _():
        state_ref[...] = jnp.zeros_like(state_ref)
        ubuf_ref[0:hist, :] = jnp.zeros((hist, ubuf_ref.shape[1]), BF16)

    ubuf_ref[hist:hist + L, :] = xbc_ref[...]
    for g in range(G):
        cs = slice(g * slab, (g + 1) * slab)
        u = ubuf_ref[:, cs]
        shifted = _dot(shift_ref[...], u)
        acc = convb_ref[:, cs] + convw_ref[SSM_CONV - 1:SSM_CONV, cs] * u[hist:].astype(F32)
        for k in range(SSM_CONV - 1):
            acc = acc + convw_ref[k:k + 1, cs] * shifted[k * L:(k + 1) * L]
        act = _silu(acc)
        x_ref[:, g * gw:(g + 1) * gw] = act[:, :gw]
        bc_ref[:, g * 2 * N:(g + 1) * 2 * N] = act[:, gw:].astype(BF16)
    ubuf_ref[0:hist, :] = ubuf_ref[L:L + hist, :]

    dt = _softplus(dt_ref[0] + dtb_ref[...])
    a = -jnp.exp(alog_ref[...])
    acum = _cumsum_lanes(dt * a)
    last = acum[:, L - 1:L]
    wrow = dt * jnp.exp(last - acum)
    acum2 = acum * LOG2E

    neg_rows = [jnp.concatenate([-p] * H, axis=1) * headdiag_ref[...] for p in _split3_bf16(acum2)]
    neg_rows.append(jnp.zeros((H, H * L), BF16))
    seg = _dot(jnp.concatenate([_split3_t(acum2), jnp.ones((L, 4 * H), BF16)], axis=1),
               jnp.concatenate([headsel_ref[...]] + neg_rows, axis=0))
    chan = _dot(jnp.concatenate([_split3_t(dt), _split3_t(wrow), _split3_t(jnp.exp(acum))], axis=0),
                chansel_ref[...])
    causal = (lax.broadcasted_iota(jnp.int32, (L, L), 0)
              >= lax.broadcasted_iota(jnp.int32, (L, L), 1))
    head_of_lane = lax.broadcasted_iota(jnp.int32, (L, gw), 1) // P

    def b_of(g):
        return bc_ref[:, g * 2 * N:g * 2 * N + N]

    def c_of(g):
        return bc_ref[:, g * 2 * N + N:(g + 1) * 2 * N]

    cb_next = _dot_nt(c_of(0), b_of(0))
    for g in range(G):
        sl = slice(g * gw, (g + 1) * gw)
        cb = cb_next
        if g + 1 < G:
            cb_next = _dot_nt(c_of(g + 1), b_of(g + 1))
        x_g, b_g, c_g = x_ref[:, sl], b_of(g), c_of(g)
        st = state_ref[g]
        y_state = _dot(c_g, st.astype(BF16))
        xw = (x_g * chan[L:2 * L, sl]).astype(BF16)
        state_ref[g] = (st * chan[3 * L - 1:3 * L, sl]
                        + lax.dot_general(b_g, xw, TN_DIMS, preferred_element_type=F32))
        m_g = jnp.concatenate(
            [(jnp.exp2(jnp.where(causal, seg[:, h * L:(h + 1) * L], -jnp.inf)) * cb).astype(BF16)
             for h in range(g * hpg, (g + 1) * hpg)], axis=1)
        xdt = (x_g * chan[0:L, sl]).astype(BF16)
        xdt_diag = jnp.concatenate(
            [jnp.where(head_of_lane == r, xdt, jnp.zeros_like(xdt)) for r in range(hpg)], axis=0)
        y = _dot(m_g, xdt_diag) + y_state * chan[2 * L:3 * L, sl] + dskip_ref[:, sl] * x_g
        y = y * _silu(z_ref[:, sl].astype(F32))
        y = y * lax.rsqrt(jnp.mean(y * y, axis=-1, keepdims=True) + RMS_EPS)
        o_ref[:, sl] = (y * normw_ref[:, sl]).astype(BF16)


def _ssd_constants(n_heads, d_inner):
    L, H = SSM_CHUNK, n_heads
    hist = 2 * V7X_SUBLANES
    t = jnp.arange(L)[:, None]
    src = jnp.arange(hist + L)[None, :]
    shift = jnp.concatenate([(src == t + hist - d).astype(BF16)
                             for d in range(SSM_CONV - 1, 0, -1)], axis=0)
    k = jnp.arange(4 * H)[:, None]
    live = k < 3 * H
    headsel = (live & (k % H == jnp.arange(H * L)[None, :] // L)).astype(BF16)
    headdiag = (jnp.arange(H)[:, None] == jnp.arange(H * L)[None, :] // L).astype(BF16)
    chansel = (live & (k % H == jnp.arange(d_inner)[None, :] // SSM_HEAD_DIM)).astype(BF16)
    return shift, headsel, headdiag, chansel, hist


def _conv_group_major(t, d_inner):
    G, N = SSM_GROUPS, SSM_STATE
    gw = d_inner // G
    parts = []
    for g in range(G):
        parts += [t[..., g * gw:(g + 1) * gw],
                  t[..., d_inner + g * N:d_inner + (g + 1) * N],
                  t[..., d_inner + (G + g) * N:d_inner + (G + g + 1) * N]]
    return jnp.concatenate(parts, axis=-1)


def _ssd(proj, dt_t, conv_w, conv_b, dt_bias, a_log, d_skip, norm_w, *, bsz, seq, d_inner, n_heads):
    conv_dim = conv_w.shape[1]
    L = SSM_CHUNK
    assert seq % L == 0 and conv_dim % d_inner == 0 and 4 * n_heads == L
    assert conv_dim == d_inner + 2 * SSM_GROUPS * SSM_STATE
    nc = seq // L
    z_blk = conv_dim // d_inner
    gw = d_inner // SSM_GROUPS
    shift, headsel, headdiag, chansel, hist = _ssd_constants(n_heads, d_inner)
    tok = lambda b, c: (b * nc + c, 0)
    return pl.pallas_call(
        functools.partial(_ssd_kernel, n_heads=n_heads),
        grid=(bsz, nc),
        in_specs=[pl.BlockSpec((L, conv_dim), tok),
                  pl.BlockSpec((L, d_inner), lambda b, c: (b * nc + c, z_blk)),
                  pl.BlockSpec((1, n_heads, L), lambda b, c: (b, 0, c)),
                  _resident((SSM_CONV, conv_dim)), _resident((1, conv_dim)),
                  _resident((n_heads, 1)), _resident((n_heads, 1)),
                  _resident((1, d_inner)), _resident((1, d_inner)),
                  _resident(shift.shape), _resident(headsel.shape), _resident(headdiag.shape),
                  _resident(chansel.shape)],
        out_specs=pl.BlockSpec((L, d_inner), tok),
        out_shape=jax.ShapeDtypeStruct((bsz * seq, d_inner), BF16),
        scratch_shapes=[pltpu.VMEM((SSM_GROUPS, SSM_STATE, gw), F32),
                        pltpu.VMEM((hist + L, conv_dim), BF16),
                        pltpu.VMEM((L, d_inner), F32),
                        pltpu.VMEM((L, conv_dim - d_inner), BF16)],
        compiler_params=_params(("parallel", "arbitrary")),
        name="ssd",
    )(proj, proj, dt_t, conv_w, conv_b.reshape(1, conv_dim),
      dt_bias.reshape(n_heads, 1), a_log.reshape(n_heads, 1),
      jnp.repeat(d_skip, SSM_HEAD_DIM).reshape(1, d_inner), norm_w.reshape(1, d_inner),
      shift, headsel, headdiag, chansel)


def _moba_kernel(q_ref, k_ref, vt_ref, o_ref, *, nb):
    blk = MOBA_BLOCK
    seq = nb * blk

    bi = lax.broadcasted_iota(jnp.int32, (nb, seq), 0)
    ti = lax.broadcasted_iota(jnp.int32, (nb, seq), 1)
    in_blk = (ti >= bi * blk) & (ti < (bi + 1) * blk)
    avg = jnp.where(in_blk, 1.0 / blk, 0.0).astype(BF16)

    key_idx = lax.broadcasted_iota(jnp.int32, (blk, blk), 0)
    qry_idx = lax.broadcasted_iota(jnp.int32, (blk, blk), 1)
    causal = key_idx <= qry_idx

    def scores(i):
        return _dot_nt(k_ref[0:(i + 1) * blk, :], q_ref[i * blk:(i + 1) * blk, :])

    ahead = 3
    s_queue = [scores(i) for i in range(min(ahead, nb))]
    k_mean = _dot(avg, k_ref[...])
    q_all = q_ref[...]
    gate3 = _dot_nt(jnp.concatenate(_split3_bf16(k_mean), axis=0), q_all)
    gate = gate3[0:nb] + gate3[nb:2 * nb] + gate3[2 * nb:3 * nb]

    d = vt_ref.shape[0] - ONES_ROWS

    for i in range(nb):
        qs = slice(i * blk, (i + 1) * blk)
        n_keys = (i + 1) * blk
        s_all = s_queue.pop(0)
        if i + ahead < nb:
            s_queue.append(scores(i + ahead))
        g_rows = [gate[j:j + 1, qs] for j in range(i)]

        blocks = []
        for j in range(i + 1):
            s = s_all[j * blk:(j + 1) * blk]
            if j == i:
                s = jnp.where(causal, s, -jnp.inf)
            elif i > MOBA_TOPK:
                rank = jnp.zeros((1, blk), jnp.int32)
                for jj in range(i):
                    if jj < j:
                        rank = rank + (g_rows[jj] >= g_rows[j]).astype(jnp.int32)
                    elif jj > j:
                        rank = rank + (g_rows[jj] > g_rows[j]).astype(jnp.int32)
                s = jnp.where(rank < MOBA_TOPK, s, -jnp.inf)
            blocks.append(s)

        m = functools.reduce(jnp.maximum, [jnp.max(s, axis=0, keepdims=True) for s in blocks])
        p_all = jnp.concatenate([jnp.exp2(s - m).astype(BF16) for s in blocks], axis=0)
        acc = _dot(vt_ref[:, 0:n_keys], p_all)
        o_ref[qs, :] = (acc[0:d] * (1.0 / acc[d:d + 1])).T.astype(BF16)


def _moba(proj, v_t, *, bsz, seq, n_heads, q_col, k_col):
    d = ATT_HEAD_DIM
    assert seq % MOBA_BLOCK == 0
    nb = seq // MOBA_BLOCK
    return pl.pallas_call(
        functools.partial(_moba_kernel, nb=nb),
        grid=(bsz, n_heads),
        in_specs=[pl.BlockSpec((seq, d), lambda b, h: (b, q_col + h)),
                  pl.BlockSpec((seq, d), lambda b, h: (b, k_col + h)),
                  pl.BlockSpec((None, None, d + ONES_ROWS, seq), lambda b, h: (b, h, 0, 0))],
        out_specs=pl.BlockSpec((seq, d), lambda b, h: (b, h)),
        out_shape=jax.ShapeDtypeStruct((bsz * seq, n_heads * d), BF16),
        compiler_params=_params(("parallel", "arbitrary")),
        name="moba",
    )(proj, proj, v_t)


def _merge_ln_kernel(ys_ref, ya_ref, gr_ref, h_ref, wso_ref, wao_ref, wo_ref, bg_ref,
                     lg_ref, lb_ref, o_ref):
    d = h_ref.shape[1]
    y_ssm = _dot(ys_ref[...], wso_ref[...])
    y_att = _dot(ya_ref[...], wao_ref[...])
    gates = jax.nn.sigmoid(gr_ref[...].astype(F32) + bg_ref[...])
    merged = gates[:, :d] * y_ssm + gates[:, d:] * y_att
    mix = _dot(merged.astype(BF16), wo_ref[...])
    o_ref[...] = _layer_norm(DEEPNORM_ALPHA * h_ref[...] + mix, lg_ref[...], lb_ref[...])


def _merge_ln(y_ssm, y_att, proj, h, w_ssm_out, w_att_out, w_o, b_gate, ln_g, ln_b, *, gate_col):
    n, d = h.shape
    d_inner, att_w = y_ssm.shape[1], y_att.shape[1]
    row = lambda i: (i, 0)
    return pl.pallas_call(
        _merge_ln_kernel,
        grid=(n // FFN_ROWS,),
        in_specs=[pl.BlockSpec((FFN_ROWS, d_inner), row),
                  pl.BlockSpec((FFN_ROWS, att_w), row),
                  pl.BlockSpec((FFN_ROWS, 2 * d), lambda i: (i, gate_col)),
                  pl.BlockSpec((FFN_ROWS, d), row),
                  _resident((d_inner, d)), _resident((att_w, d)), _resident((d, d)),
                  _resident((1, 2 * d)), _resident((1, d)), _resident((1, d))],
        out_specs=pl.BlockSpec((FFN_ROWS, d), row),
        out_shape=jax.ShapeDtypeStruct((n, d), F32),
        compiler_params=_params(("parallel",)),
        name="merge_ln",
    )(y_ssm, y_att, proj, h, w_ssm_out.astype(BF16), w_att_out.astype(BF16), w_o.astype(BF16),
      b_gate.reshape(1, 2 * d), ln_g.reshape(1, d), ln_b.reshape(1, d))


def kernel(x, ffn1_w_gate, ffn1_w_up, ffn1_w_down, ln1_g, ln1_b, w_in, conv_w, conv_b, dt_bias,
           a_log, d_skip, ssm_norm_w, w_ssm_out, w_att_out, b_gate, w_o, ln2_g, ln2_b,
           ffn2_w_gate, ffn2_w_up, ffn2_w_down, ln3_g, ln3_b):
    bsz, seq, d = x.shape
    n = bsz * seq
    d_inner = w_ssm_out.shape[0]
    conv_dim = conv_w.shape[1]
    n_ssm_heads = dt_bias.shape[0]
    att_w = w_att_out.shape[0]
    n_att_heads = att_w // ATT_HEAD_DIM

    edges = [0]
    for width in (d_inner, conv_dim, n_ssm_heads, att_w, att_w, att_w, 2 * d):
        edges.append(edges[-1] + width)
    assert edges[-1] == w_in.shape[1]
    w_z, w_xbc, w_dt, w_q, w_k, w_v, w_g = (w_in[:, lo:hi] for lo, hi in zip(edges[:-1], edges[1:]))
    w_q = w_q * (LOG2E / math.sqrt(ATT_HEAD_DIM))
    w_xbc = _conv_group_major(w_xbc, d_inner)
    w_main = jnp.concatenate([w_xbc, w_z, w_q, w_k, w_g], axis=1).astype(BF16)
    q_off = conv_dim + d_inner
    k_off = q_off + att_w
    g_off = k_off + att_w
    assert q_off % ATT_HEAD_DIM == 0 and g_off % (2 * d) == 0

    h1, h1b = _ffn_ln(x.reshape(n, d), ffn1_w_gate, ffn1_w_up, ffn1_w_down, ln1_g, ln1_b,
                      with_bf16_out=True)
    proj = _proj(h1b, w_main)
    v_t, dt_t = _proj_t(h1b, w_v.T.astype(BF16), w_dt.T.astype(BF16), bsz, seq)
    y_ssm = _ssd(proj, dt_t, _conv_group_major(conv_w, d_inner), _conv_group_major(conv_b, d_inner),
                 dt_bias, a_log, d_skip, ssm_norm_w,
                 bsz=bsz, seq=seq, d_inner=d_inner, n_heads=n_ssm_heads)
    y_att = _moba(proj, v_t, bsz=bsz, seq=seq, n_heads=n_att_heads,
                  q_col=q_off // ATT_HEAD_DIM, k_col=k_off // ATT_HEAD_DIM)
    h2 = _merge_ln(y_ssm, y_att, proj, h1, w_ssm_out, w_att_out, w_o, b_gate, ln2_g, ln2_b,
                   gate_col=g_off // (2 * d))
    (out,) = _ffn_ln(h2, ffn2_w_gate, ffn2_w_up, ffn2_w_down, ln3_g, ln3_b, with_bf16_out=False)
    return out.reshape(bsz, seq, d)
```

```python
import functools
import math

import jax
import jax.numpy as jnp
from jax import lax
from jax.experimental import pallas as pl
from jax.experimental.pallas import tpu as pltpu

F32 = jnp.float32
BF16 = jnp.bfloat16

SSM_HEAD_DIM = 64
SSM_GROUPS = 8
SSM_STATE = 128
SSM_CONV = 4
SSM_CHUNK = 128
ATT_HEAD_DIM = 128
MOBA_BLOCK = 256
MOBA_TOPK = 3
DEPTH = 1
DEEPNORM_ALPHA = (2 * DEPTH) ** 0.25
LN_EPS = 1e-5
RMS_EPS = 1e-5

V7X_VMEM_BYTES = 64 * 1024 * 1024
V7X_SUBLANES = 8
VMEM_LIMIT_BYTES = V7X_VMEM_BYTES - 8 * 1024 * 1024

FFN_ROWS = 512
FFN_CHUNK = 256
PROJ_ROWS = 512
PROJ_COLS = 1024
SSM_STEP_CHUNKS = 4

NT_DIMS = (((1,), (1,)), ((), ()))
TN_DIMS = (((0,), (0,)), ((), ()))
LOG2E = 1.4426950408889634


def _params(semantics):
    return pltpu.CompilerParams(dimension_semantics=semantics, vmem_limit_bytes=VMEM_LIMIT_BYTES)


def _resident(shape):
    nd = len(shape)
    return pl.BlockSpec(shape, lambda *_: (0,) * nd, pipeline_mode=pl.Buffered(1))


def _dot(a, b):
    return jnp.dot(a, b, preferred_element_type=F32)


def _dot_nt(a, b):
    return lax.dot_general(a, b, NT_DIMS, preferred_element_type=F32)


def _silu(v):
    return v * jax.nn.sigmoid(v)


def _silu_of_twice(h):
    return h * (1.0 + jnp.tanh(h))


def _layer_norm(r, g, b):
    mu = jnp.mean(r, axis=-1, keepdims=True)
    d = r - mu
    var = jnp.mean(d * d, axis=-1, keepdims=True)
    return d * lax.rsqrt(var + LN_EPS) * g + b


def _ffn_ln_kernel(x_ref, wg_ref, wu_ref, wd_ref, g_ref, b_ref, *refs, d_ff, with_bf16_out):
    if with_bf16_out:
        o_ref, ob_ref, act_ref = refs
    else:
        o_ref, act_ref = refs
    x = x_ref[...]
    xb = x.astype(BF16)
    for c in range(d_ff // FFN_CHUNK):
        sl = slice(c * FFN_CHUNK, (c + 1) * FFN_CHUNK)
        gate = _dot(xb, wg_ref[:, sl])
        up = _dot(xb, wu_ref[:, sl])
        act_ref[:, sl] = (_silu(gate) * up).astype(BF16)
    y = _dot(act_ref[...], wd_ref[...])
    out = _layer_norm(DEEPNORM_ALPHA * x + 0.5 * y, g_ref[...], b_ref[...])
    o_ref[...] = out
    if with_bf16_out:
        ob_ref[...] = out.astype(BF16)


def _ffn_ln(x, wg, wu, wd, g, b, *, with_bf16_out):
    n, d = x.shape
    d_ff = wg.shape[1]
    assert n % FFN_ROWS == 0 and d_ff % FFN_CHUNK == 0
    row = lambda i: (i, 0)
    out_shape = [jax.ShapeDtypeStruct((n, d), F32)]
    out_specs = [pl.BlockSpec((FFN_ROWS, d), row)]
    if with_bf16_out:
        out_shape.append(jax.ShapeDtypeStruct((n, d), BF16))
        out_specs.append(pl.BlockSpec((FFN_ROWS, d), row))
    return pl.pallas_call(
        functools.partial(_ffn_ln_kernel, d_ff=d_ff, with_bf16_out=with_bf16_out),
        grid=(n // FFN_ROWS,),
        in_specs=[pl.BlockSpec((FFN_ROWS, d), row),
                  _resident((d, d_ff)), _resident((d, d_ff)), _resident((d_ff, d)),
                  _resident((1, d)), _resident((1, d))],
        out_specs=out_specs,
        out_shape=out_shape,
        scratch_shapes=[pltpu.VMEM((FFN_ROWS, d_ff), BF16)],
        compiler_params=_params(("parallel",)),
        name="ffn_ln",
    )(x, wg.astype(BF16), wu.astype(BF16), wd.astype(BF16), g.reshape(1, d), b.reshape(1, d))


ONES_ROWS = 2 * V7X_SUBLANES


def _proj_kernel(x_ref, w_ref, wv_ref, wdt_ref, o_ref, vt_ref, dtt_ref):
    x = x_ref[...]
    for j in range(w_ref.shape[1] // PROJ_COLS):
        cs = slice(j * PROJ_COLS, (j + 1) * PROJ_COLS)
        o_ref[:, cs] = _dot(x, w_ref[:, cs]).astype(BF16)
    n_heads, rows, cols = vt_ref.shape[1:]
    hd = rows - ONES_ROWS
    v_t = _dot_nt(wv_ref[...], x).astype(BF16)
    for h in range(n_heads):
        vt_ref[0, h, 0:hd, :] = v_t[h * hd:(h + 1) * hd]
        vt_ref[0, h, hd:rows, :] = jnp.ones((ONES_ROWS, cols), BF16)
    dtt_ref[0] = _dot_nt(wdt_ref[...], x)


def _proj(xb, w, w_vt, w_dtt, bsz, seq):
    n, d = xb.shape
    cols = w.shape[1]
    cv, cdt = w_vt.shape[0], w_dtt.shape[0]
    assert seq % PROJ_ROWS == 0 and cols % PROJ_COLS == 0 and cv % ATT_HEAD_DIM == 0
    spb = seq // PROJ_ROWS
    n_heads = cv // ATT_HEAD_DIM
    rows = ATT_HEAD_DIM + ONES_ROWS
    return pl.pallas_call(
        _proj_kernel,
        grid=(bsz, spb),
        in_specs=[pl.BlockSpec((PROJ_ROWS, d), lambda b, s: (b * spb + s, 0)),
                  _resident((d, cols)), _resident((cv, d)), _resident((cdt, d))],
        out_specs=[pl.BlockSpec((PROJ_ROWS, cols), lambda b, s: (b * spb + s, 0)),
                   pl.BlockSpec((1, n_heads, rows, PROJ_ROWS), lambda b, s: (b, 0, 0, s)),
                   pl.BlockSpec((1, cdt, PROJ_ROWS), lambda b, s: (b, 0, s))],
        out_shape=[jax.ShapeDtypeStruct((n, cols), BF16),
                   jax.ShapeDtypeStruct((bsz, n_heads, rows, seq), BF16),
                   jax.ShapeDtypeStruct((bsz, cdt, seq), F32)],
        compiler_params=_params(("parallel", "arbitrary")),
        name="in_proj",
    )(xb, w, w_vt, w_dtt)


def _cumsum_lanes(v):
    n = v.shape[-1]
    lane = lax.broadcasted_iota(jnp.int32, v.shape, v.ndim - 1)
    shift = 1
    while shift < n:
        v = v + jnp.where(lane >= shift, pltpu.roll(v, shift, v.ndim - 1), 0.0)
        shift *= 2
    return v


def _softplus(v):
    return jnp.maximum(v, 0.0) + jnp.log1p(jnp.exp(-jnp.abs(v)))


def _split3_bf16(v):
    hi = v.astype(BF16)
    r1 = v - hi.astype(F32)
    mid = r1.astype(BF16)
    lo = (r1 - mid.astype(F32)).astype(BF16)
    return hi, mid, lo


def _split3_t(v):
    parts = [p.astype(F32) for p in _split3_bf16(v)] + [jnp.zeros_like(v)]
    return jnp.concatenate(parts, axis=0).T.astype(BF16)


def _ssd_kernel(xbc_ref, z_ref, dt_ref, convw_ref, convb_ref, dtb_ref, alog_ref, dskip_ref,
                normw_ref, shift_ref, headsel_ref, headdiag_ref, chansel_ref,
                o_ref, state_ref, ubuf_ref, x_ref, bc_ref, *, n_heads):
    L, G = SSM_CHUNK, SSM_GROUPS
    hist = ubuf_ref.shape[1] - L

    @pl.when(pl.program_id(1) == 0)
    def _():
        state_ref[...] = jnp.zeros_like(state_ref)
        ubuf_ref[0, 0:hist, :] = jnp.zeros((hist, ubuf_ref.shape[2]), BF16)

    causal = (lax.broadcasted_iota(jnp.int32, (L, L), 0)
              >= lax.broadcasted_iota(jnp.int32, (L, L), 1))
    gw = x_ref.shape[2] // G
    head_of_lane = lax.broadcasted_iota(jnp.int32, (L, gw), 1) // SSM_HEAD_DIM
    a = -jnp.exp(alog_ref[...])

    for ci in range(SSM_STEP_CHUNKS):
        rows = slice(ci * L, (ci + 1) * L)
        nxt = (ci + 1) % SSM_STEP_CHUNKS
        _ssd_chunk(xbc_ref.at[rows], z_ref.at[rows], dt_ref[0, :, rows], o_ref.at[rows],
                   ubuf_ref.at[ci], ubuf_ref.at[nxt], x_ref.at[ci], bc_ref.at[ci],
                   convw_ref, convb_ref, dtb_ref, a, dskip_ref, normw_ref, shift_ref,
                   headsel_ref, headdiag_ref, chansel_ref, state_ref, causal, head_of_lane,
                   n_heads=n_heads)


def _ssd_chunk(xbc_ref, z_ref, dt_raw, o_ref, ubuf_ref, ubuf_next_ref, x_ref, bc_ref,
               convw_ref, convb_ref, dtb_ref, a, dskip_ref, normw_ref, shift_ref,
               headsel_ref, headdiag_ref, chansel_ref, state_ref, causal, head_of_lane, *, n_heads):
    L, P, N, G = SSM_CHUNK, SSM_HEAD_DIM, SSM_STATE, SSM_GROUPS
    H = n_heads
    hpg = H // G
    gw = hpg * P
    slab = gw + 2 * N
    hist = ubuf_ref.shape[0] - L

    ubuf_ref[hist:hist + L, :] = xbc_ref[...]
    for g in range(G):
        cs = slice(g * slab, (g + 1) * slab)
        u = ubuf_ref[:, cs]
        shifted = _dot(shift_ref[...], u)
        acc = convb_ref[:, cs] + convw_ref[SSM_CONV - 1:SSM_CONV, cs] * u[hist:].astype(F32)
        for k in range(SSM_CONV - 1):
            acc = acc + convw_ref[k:k + 1, cs] * shifted[k * L:(k + 1) * L]
        act = _silu_of_twice(acc)
        x_ref[:, g * gw:(g + 1) * gw] = act[:, :gw]
        bc_ref[:, g * 2 * N:(g + 1) * 2 * N] = act[:, gw:].astype(BF16)
    ubuf_next_ref[0:hist, :] = ubuf_ref[L:L + hist, :]

    dt = _softplus(dt_raw + dtb_ref[...])
    acum = _cumsum_lanes(dt * a)
    last = acum[:, L - 1:L]
    wrow = dt * jnp.exp(last - acum)
    acum2 = acum * LOG2E

    neg_rows = [jnp.concatenate([-p] * H, axis=1) * headdiag_ref[...] for p in _split3_bf16(acum2)]
    neg_rows.append(jnp.zeros((H, H * L), BF16))
    seg_lhs = jnp.concatenate([_split3_t(acum2), jnp.ones((L, 4 * H), BF16)], axis=1)
    chan_lhs = jnp.concatenate([_split3_t(dt), _split3_t(wrow), _split3_t(jnp.exp(acum))], axis=0)

    def b_of(g):
        return bc_ref[:, g * 2 * N:g * 2 * N + N]

    def c_of(g):
        return bc_ref[:, g * 2 * N + N:(g + 1) * 2 * N]

    def expand(g):
        hs = slice(g * hpg * L, (g + 1) * hpg * L)
        seg_rhs = jnp.concatenate([headsel_ref[:, hs]] + [r[:, hs] for r in neg_rows], axis=0)
        return (_dot(seg_lhs, seg_rhs), _dot(chan_lhs, chansel_ref[:, g * gw:(g + 1) * gw]),
                _dot_nt(c_of(g), b_of(g)))

    expanded = expand(0)
    for g in range(G):
        sl = slice(g * gw, (g + 1) * gw)
        seg, chan, cb = expanded
        if g + 1 < G:
            expanded = expand(g + 1)
        x_g, b_g, c_g = x_ref[:, sl], b_of(g), c_of(g)
        st = state_ref[g]
        y_state = _dot(c_g, st.astype(BF16))
        xw = (x_g * chan[L:2 * L]).astype(BF16)
        state_ref[g] = (st * chan[3 * L - 1:3 * L]
                        + lax.dot_general(b_g, xw, TN_DIMS, preferred_element_type=F32))
        m_g = jnp.concatenate(
            [(jnp.exp2(jnp.where(causal, seg[:, r * L:(r + 1) * L], -jnp.inf)) * cb).astype(BF16)
             for r in range(hpg)], axis=1)
        xdt = (x_g * chan[0:L]).astype(BF16)
        xdt_diag = jnp.concatenate(
            [jnp.where(head_of_lane == r, xdt, jnp.zeros_like(xdt)) for r in range(hpg)], axis=0)
        y = _dot(m_g, xdt_diag) + y_state * chan[2 * L:3 * L] + dskip_ref[:, sl] * x_g
        y = y * _silu_of_twice(z_ref[:, sl].astype(F32))
        y = y * lax.rsqrt(jnp.mean(y * y, axis=-1, keepdims=True) + RMS_EPS)
        o_ref[:, sl] = (y * normw_ref[:, sl]).astype(BF16)


def _ssd_constants(n_heads, d_inner):
    L, H = SSM_CHUNK, n_heads
    hist = 2 * V7X_SUBLANES
    t = jnp.arange(L)[:, None]
    src = jnp.arange(hist + L)[None, :]
    shift = jnp.concatenate([(src == t + hist - d).astype(BF16)
                             for d in range(SSM_CONV - 1, 0, -1)], axis=0)
    k = jnp.arange(4 * H)[:, None]
    live = k < 3 * H
    headsel = (live & (k % H == jnp.arange(H * L)[None, :] // L)).astype(BF16)
    headdiag = (jnp.arange(H)[:, None] == jnp.arange(H * L)[None, :] // L).astype(BF16)
    chansel = (live & (k % H == jnp.arange(d_inner)[None, :] // SSM_HEAD_DIM)).astype(BF16)
    return shift, headsel, headdiag, chansel, hist


def _conv_group_major(t, d_inner):
    G, N = SSM_GROUPS, SSM_STATE
    lead = t.shape[:-1]
    x = t[..., :d_inner].reshape(lead + (G, d_inner // G))
    b = t[..., d_inner:d_inner + G * N].reshape(lead + (G, N))
    c = t[..., d_inner + G * N:].reshape(lead + (G, N))
    return jnp.concatenate([x, b, c], axis=-1).reshape(t.shape)


def _ssd(proj, dt_t, conv_w, conv_b, dt_bias, a_log, d_skip, norm_w, *, bsz, seq, d_inner, n_heads):
    conv_dim = conv_w.shape[1]
    L = SSM_CHUNK
    rows = SSM_STEP_CHUNKS * L
    assert seq % rows == 0 and conv_dim % d_inner == 0 and 4 * n_heads == L
    assert conv_dim == d_inner + 2 * SSM_GROUPS * SSM_STATE
    nc = seq // rows
    z_blk = conv_dim // d_inner
    gw = d_inner // SSM_GROUPS
    shift, headsel, headdiag, chansel, hist = _ssd_constants(n_heads, d_inner)
    tok = lambda b, c: (b * nc + c, 0)
    return pl.pallas_call(
        functools.partial(_ssd_kernel, n_heads=n_heads),
        grid=(bsz, nc),
        in_specs=[pl.BlockSpec((rows, conv_dim), tok),
                  pl.BlockSpec((rows, d_inner), lambda b, c: (b * nc + c, z_blk)),
                  pl.BlockSpec((1, n_heads, rows), lambda b, c: (b, 0, c)),
                  _resident((SSM_CONV, conv_dim)), _resident((1, conv_dim)),
                  _resident((n_heads, 1)), _resident((n_heads, 1)),
                  _resident((1, d_inner)), _resident((1, d_inner)),
                  _resident(shift.shape), _resident(headsel.shape), _resident(headdiag.shape),
                  _resident(chansel.shape)],
        out_specs=pl.BlockSpec((rows, d_inner), tok),
        out_shape=jax.ShapeDtypeStruct((bsz * seq, d_inner), BF16),
        scratch_shapes=[pltpu.VMEM((SSM_GROUPS, SSM_STATE, gw), F32),
                        pltpu.VMEM((SSM_STEP_CHUNKS, hist + L, conv_dim), BF16),
                        pltpu.VMEM((SSM_STEP_CHUNKS, L, d_inner), F32),
                        pltpu.VMEM((SSM_STEP_CHUNKS, L, conv_dim - d_inner), BF16)],
        compiler_params=_params(("parallel", "arbitrary")),
        name="ssd",
    )(proj, proj, dt_t, conv_w, conv_b.reshape(1, conv_dim),
      dt_bias.reshape(n_heads, 1), a_log.reshape(n_heads, 1),
      jnp.repeat(d_skip, SSM_HEAD_DIM).reshape(1, d_inner), norm_w.reshape(1, d_inner),
      shift, headsel, headdiag, chansel)


def _moba_kernel(q_ref, k_ref, vt_ref, o_ref, *, nb):
    blk = MOBA_BLOCK
    seq = nb * blk

    bi = lax.broadcasted_iota(jnp.int32, (nb, seq), 0)
    ti = lax.broadcasted_iota(jnp.int32, (nb, seq), 1)
    in_blk = (ti >= bi * blk) & (ti < (bi + 1) * blk)
    avg = jnp.where(in_blk, 1.0 / blk, 0.0).astype(BF16)

    key_idx = lax.broadcasted_iota(jnp.int32, (blk, blk), 0)
    qry_idx = lax.broadcasted_iota(jnp.int32, (blk, blk), 1)
    causal = key_idx <= qry_idx

    def scores(i):
        return _dot_nt(k_ref[0:(i + 1) * blk, :], q_ref[i * blk:(i + 1) * blk, :])

    ahead = 3
    s_queue = [scores(i) for i in range(min(ahead, nb))]
    k_mean = _dot(avg, k_ref[...])
    q_all = q_ref[...]
    gate3 = _dot_nt(jnp.concatenate(_split3_bf16(k_mean), axis=0), q_all)
    gate = gate3[0:nb] + gate3[nb:2 * nb] + gate3[2 * nb:3 * nb]

    d = vt_ref.shape[0] - ONES_ROWS

    for i in range(nb):
        qs = slice(i * blk, (i + 1) * blk)
        n_keys = (i + 1) * blk
        s_all = s_queue.pop(0)
        if i + ahead < nb:
            s_queue.append(scores(i + ahead))
        g_rows = [gate[j:j + 1, qs] for j in range(i)]

        blocks = []
        for j in range(i + 1):
            s = s_all[j * blk:(j + 1) * blk]
            if j == i:
                s = jnp.where(causal, s, -jnp.inf)
            elif i > MOBA_TOPK:
                rank = jnp.zeros((1, blk), jnp.int32)
                for jj in range(i):
                    if jj < j:
                        rank = rank + (g_rows[jj] >= g_rows[j]).astype(jnp.int32)
                    elif jj > j:
                        rank = rank + (g_rows[jj] > g_rows[j]).astype(jnp.int32)
                s = jnp.where(rank < MOBA_TOPK, s, -jnp.inf)
            blocks.append(s)

        m = functools.reduce(jnp.maximum, [jnp.max(s, axis=0, keepdims=True) for s in blocks])
        p_all = jnp.concatenate([jnp.exp2(s - m).astype(BF16) for s in blocks], axis=0)
        acc = _dot(vt_ref[:, 0:n_keys], p_all)
        o_ref[qs, :] = (acc[0:d] * (1.0 / acc[d:d + 1])).T.astype(BF16)


def _moba(proj, v_t, *, bsz, seq, n_heads, q_col, k_col):
    d = ATT_HEAD_DIM
    assert seq % MOBA_BLOCK == 0
    nb = seq // MOBA_BLOCK
    return pl.pallas_call(
        functools.partial(_moba_kernel, nb=nb),
        grid=(bsz, n_heads),
        in_specs=[pl.BlockSpec((seq, d), lambda b, h: (b, q_col + h)),
                  pl.BlockSpec((seq, d), lambda b, h: (b, k_col + h)),
                  pl.BlockSpec((None, None, d + ONES_ROWS, seq), lambda b, h: (b, h, 0, 0))],
        out_specs=pl.BlockSpec((seq, d), lambda b, h: (b, h)),
        out_shape=jax.ShapeDtypeStruct((bsz * seq, n_heads * d), BF16),
        compiler_params=_params(("parallel", "arbitrary")),
        name="moba",
    )(proj, proj, v_t)


def _merge_ln_kernel(ys_ref, ya_ref, gr_ref, h_ref, wso_ref, wao_ref, wo_ref, bg_ref,
                     lg_ref, lb_ref, o_ref):
    d = h_ref.shape[1]
    y_ssm = _dot(ys_ref[...], wso_ref[...])
    y_att = _dot(ya_ref[...], wao_ref[...])
    gates = jax.nn.sigmoid(gr_ref[...].astype(F32) + bg_ref[...])
    merged = gates[:, :d] * y_ssm + gates[:, d:] * y_att
    mix = _dot(merged.astype(BF16), wo_ref[...])
    o_ref[...] = _layer_norm(DEEPNORM_ALPHA * h_ref[...] + mix, lg_ref[...], lb_ref[...])


def _merge_ln(y_ssm, y_att, proj, h, w_ssm_out, w_att_out, w_o, b_gate, ln_g, ln_b, *, gate_col):
    n, d = h.shape
    d_inner, att_w = y_ssm.shape[1], y_att.shape[1]
    row = lambda i: (i, 0)
    return pl.pallas_call(
        _merge_ln_kernel,
        grid=(n // FFN_ROWS,),
        in_specs=[pl.BlockSpec((FFN_ROWS, d_inner), row),
                  pl.BlockSpec((FFN_ROWS, att_w), row),
                  pl.BlockSpec((FFN_ROWS, 2 * d), lambda i: (i, gate_col)),
                  pl.BlockSpec((FFN_ROWS, d), row),
                  _resident((d_inner, d)), _resident((att_w, d)), _resident((d, d)),
                  _resident((1, 2 * d)), _resident((1, d)), _resident((1, d))],
        out_specs=pl.BlockSpec((FFN_ROWS, d), row),
        out_shape=jax.ShapeDtypeStruct((n, d), F32),
        compiler_params=_params(("parallel",)),
        name="merge_ln",
    )(y_ssm, y_att, proj, h, w_ssm_out.astype(BF16), w_att_out.astype(BF16), w_o.astype(BF16),
      b_gate.reshape(1, 2 * d), ln_g.reshape(1, d), ln_b.reshape(1, d))


def kernel(x, ffn1_w_gate, ffn1_w_up, ffn1_w_down, ln1_g, ln1_b, w_in, conv_w, conv_b, dt_bias,
           a_log, d_skip, ssm_norm_w, w_ssm_out, w_att_out, b_gate, w_o, ln2_g, ln2_b,
           ffn2_w_gate, ffn2_w_up, ffn2_w_down, ln3_g, ln3_b):
    bsz, seq, d = x.shape
    n = bsz * seq
    d_inner = w_ssm_out.shape[0]
    conv_dim = conv_w.shape[1]
    n_ssm_heads = dt_bias.shape[0]
    att_w = w_att_out.shape[0]
    n_att_heads = att_w // ATT_HEAD_DIM

    edges = [0]
    for width in (d_inner, conv_dim, n_ssm_heads, att_w, att_w, att_w, 2 * d):
        edges.append(edges[-1] + width)
    assert edges[-1] == w_in.shape[1]
    w_z, w_xbc, w_dt, w_q, w_k, w_v, w_g = (w_in[:, lo:hi] for lo, hi in zip(edges[:-1], edges[1:]))
    w_q = w_q * (LOG2E / math.sqrt(ATT_HEAD_DIM))
    w_xbc = _conv_group_major(w_xbc, d_inner)
    w_main = jnp.concatenate([w_xbc, 0.5 * w_z, w_q, w_k, w_g], axis=1).astype(BF16)
    conv_w_half = _conv_group_major(0.5 * conv_w, d_inner)
    conv_b_half = _conv_group_major(0.5 * conv_b, d_inner)
    q_off = conv_dim + d_inner
    k_off = q_off + att_w
    g_off = k_off + att_w
    assert q_off % ATT_HEAD_DIM == 0 and g_off % (2 * d) == 0

    h1, h1b = _ffn_ln(x.reshape(n, d), ffn1_w_gate, ffn1_w_up, ffn1_w_down, ln1_g, ln1_b,
                      with_bf16_out=True)
    proj, v_t, dt_t = _proj(h1b, w_main, w_v.T.astype(BF16), w_dt.T.astype(BF16), bsz, seq)
    y_ssm = _ssd(proj, dt_t, conv_w_half, conv_b_half, dt_bias, a_log, d_skip, ssm_norm_w,
                 bsz=bsz, seq=seq, d_inner=d_inner, n_heads=n_ssm_heads)
    y_att = _moba(proj, v_t, bsz=bsz, seq=seq, n_heads=n_att_heads,
                  q_col=q_off // ATT_HEAD_DIM, k_col=k_off // ATT_HEAD_DIM)
    h2 = _merge_ln(y_ssm, y_att, proj, h1, w_ssm_out, w_att_out, w_o, b_gate, ln2_g, ln2_b,
                   gate_col=g_off // (2 * d))
    (out,) = _ffn_ln(h2, ffn2_w_gate, ffn2_w_up, ffn2_w_down, ln3_g, ln3_b, with_bf16_out=False)
    return out.reshape(bsz, seq, d)
```

```python
import functools
import math

import jax
import jax.numpy as jnp
from jax import lax
from jax.experimental import pallas as pl
from jax.experimental.pallas import tpu as pltpu

F32 = jnp.float32
BF16 = jnp.bfloat16

SSM_HEAD_DIM = 64
SSM_GROUPS = 8
SSM_STATE = 128
SSM_CONV = 4
SSM_CHUNK = 128
ATT_HEAD_DIM = 128
MOBA_BLOCK = 256
MOBA_TOPK = 3
DEPTH = 1
DEEPNORM_ALPHA = (2 * DEPTH) ** 0.25
LN_EPS = 1e-5
RMS_EPS = 1e-5

V7X_VMEM_BYTES = 64 * 1024 * 1024
V7X_SUBLANES = 8
VMEM_LIMIT_BYTES = V7X_VMEM_BYTES - 8 * 1024 * 1024

FFN_ROWS = 1024
FFN_SUB_ROWS = 512
FFN_CHUNK = 256
PROJ_ROWS = 512
PROJ_COLS = 1024
SSM_STEP_CHUNKS = 4
MOBA_STEP_HEADS = 4

NT_DIMS = (((1,), (1,)), ((), ()))
TN_DIMS = (((0,), (0,)), ((), ()))
LOG2E = 1.4426950408889634


def _params(semantics):
    return pltpu.CompilerParams(dimension_semantics=semantics, vmem_limit_bytes=VMEM_LIMIT_BYTES)


def _resident(shape):
    nd = len(shape)
    return pl.BlockSpec(shape, lambda *_: (0,) * nd, pipeline_mode=pl.Buffered(1))


def _dot(a, b):
    return jnp.dot(a, b, preferred_element_type=F32)


def _dot_nt(a, b):
    return lax.dot_general(a, b, NT_DIMS, preferred_element_type=F32)


def _silu(v):
    return v * jax.nn.sigmoid(v)


def _silu_of_twice(h):
    return h * (1.0 + jnp.tanh(h))


def _sub_tiles(rows):
    assert rows % FFN_SUB_ROWS == 0
    return [slice(r, r + FFN_SUB_ROWS) for r in range(0, rows, FFN_SUB_ROWS)]


def _layer_norm(r, g, b):
    mu = jnp.mean(r, axis=-1, keepdims=True)
    d = r - mu
    var = jnp.mean(d * d, axis=-1, keepdims=True)
    return d * lax.rsqrt(var + LN_EPS) * g + b


def _ffn_ln_kernel(x_ref, wg_ref, wu_ref, wd_ref, g_ref, b_ref, *refs, d_ff, with_bf16_out):
    if with_bf16_out:
        o_ref, ob_ref, act_ref = refs
    else:
        o_ref, act_ref = refs
    for rs in _sub_tiles(x_ref.shape[0]):
        x = x_ref[rs, :]
        xb = x.astype(BF16)
        for c in range(d_ff // FFN_CHUNK):
            sl = slice(c * FFN_CHUNK, (c + 1) * FFN_CHUNK)
            gate = _dot(xb, wg_ref[:, sl])
            up = _dot(xb, wu_ref[:, sl])
            act_ref[rs, sl] = (_silu(gate) * up).astype(BF16)
        y = _dot(act_ref[rs, :], wd_ref[...])
        out = _layer_norm(DEEPNORM_ALPHA * x + 0.5 * y, g_ref[...], b_ref[...])
        o_ref[rs, :] = out
        if with_bf16_out:
            ob_ref[rs, :] = out.astype(BF16)


def _ffn_ln(x, wg, wu, wd, g, b, *, with_bf16_out):
    n, d = x.shape
    d_ff = wg.shape[1]
    assert n % FFN_ROWS == 0 and d_ff % FFN_CHUNK == 0
    row = lambda i: (i, 0)
    out_shape = [jax.ShapeDtypeStruct((n, d), F32)]
    out_specs = [pl.BlockSpec((FFN_ROWS, d), row)]
    if with_bf16_out:
        out_shape.append(jax.ShapeDtypeStruct((n, d), BF16))
        out_specs.append(pl.BlockSpec((FFN_ROWS, d), row))
    return pl.pallas_call(
        functools.partial(_ffn_ln_kernel, d_ff=d_ff, with_bf16_out=with_bf16_out),
        grid=(n // FFN_ROWS,),
        in_specs=[pl.BlockSpec((FFN_ROWS, d), row),
                  _resident((d, d_ff)), _resident((d, d_ff)), _resident((d_ff, d)),
                  _resident((1, d)), _resident((1, d))],
        out_specs=out_specs,
        out_shape=out_shape,
        scratch_shapes=[pltpu.VMEM((FFN_ROWS, d_ff), BF16)],
        compiler_params=_params(("parallel",)),
        name="ffn_ln",
    )(x, wg.astype(BF16), wu.astype(BF16), wd.astype(BF16), g.reshape(1, d), b.reshape(1, d))


ONES_ROWS = 2 * V7X_SUBLANES


def _proj_kernel(x_ref, w_ref, wq_ref, wv_ref, wdt_ref, o_ref, qt_ref, vt_ref, dtt_ref):
    x = x_ref[...]
    for j in range(w_ref.shape[1] // PROJ_COLS):
        cs = slice(j * PROJ_COLS, (j + 1) * PROJ_COLS)
        o_ref[:, cs] = _dot(x, w_ref[:, cs]).astype(BF16)
    n_heads, rows, cols = vt_ref.shape[1:]
    hd = rows - ONES_ROWS
    q_t = _dot_nt(wq_ref[...], x).astype(BF16)
    v_t = _dot_nt(wv_ref[...], x).astype(BF16)
    for h in range(n_heads):
        qt_ref[0, h] = q_t[h * hd:(h + 1) * hd]
        vt_ref[0, h, 0:hd, :] = v_t[h * hd:(h + 1) * hd]
        vt_ref[0, h, hd:rows, :] = jnp.ones((ONES_ROWS, cols), BF16)
    dtt_ref[0] = _dot_nt(wdt_ref[...], x)


def _proj(xb, w, w_qt, w_vt, w_dtt, bsz, seq):
    n, d = xb.shape
    cols = w.shape[1]
    cv, cdt = w_vt.shape[0], w_dtt.shape[0]
    assert seq % PROJ_ROWS == 0 and cols % PROJ_COLS == 0 and cv % ATT_HEAD_DIM == 0 and w_qt.shape == w_vt.shape
    spb = seq // PROJ_ROWS
    n_heads = cv // ATT_HEAD_DIM
    rows = ATT_HEAD_DIM + ONES_ROWS
    return pl.pallas_call(
        _proj_kernel,
        grid=(bsz, spb),
        in_specs=[pl.BlockSpec((PROJ_ROWS, d), lambda b, s: (b * spb + s, 0)),
                  _resident((d, cols)), _resident((cv, d)), _resident((cv, d)), _resident((cdt, d))],
        out_specs=[pl.BlockSpec((PROJ_ROWS, cols), lambda b, s: (b * spb + s, 0)),
                   pl.BlockSpec((1, n_heads, ATT_HEAD_DIM, PROJ_ROWS), lambda b, s: (b, 0, 0, s)),
                   pl.BlockSpec((1, n_heads, rows, PROJ_ROWS), lambda b, s: (b, 0, 0, s)),
                   pl.BlockSpec((1, cdt, PROJ_ROWS), lambda b, s: (b, 0, s))],
        out_shape=[jax.ShapeDtypeStruct((n, cols), BF16),
                   jax.ShapeDtypeStruct((bsz, n_heads, ATT_HEAD_DIM, seq), BF16),
                   jax.ShapeDtypeStruct((bsz, n_heads, rows, seq), BF16),
                   jax.ShapeDtypeStruct((bsz, cdt, seq), F32)],
        compiler_params=_params(("parallel", "arbitrary")),
        name="in_proj",
    )(xb, w, w_qt, w_vt, w_dtt)


def _cumsum_lanes(v):
    n = v.shape[-1]
    lane = lax.broadcasted_iota(jnp.int32, v.shape, v.ndim - 1)
    shift = 1
    while shift < n:
        v = v + jnp.where(lane >= shift, pltpu.roll(v, shift, v.ndim - 1), 0.0)
        shift *= 2
    return v


def _softplus(v):
    return jnp.maximum(v, 0.0) + jnp.log1p(jnp.exp(-jnp.abs(v)))


def _split3_bf16(v):
    hi = v.astype(BF16)
    r1 = v - hi.astype(F32)
    mid = r1.astype(BF16)
    lo = (r1 - mid.astype(F32)).astype(BF16)
    return hi, mid, lo


def _split3_t(v):
    parts = [p.astype(F32) for p in _split3_bf16(v)] + [jnp.zeros_like(v)]
    return jnp.concatenate(parts, axis=0).T.astype(BF16)


def _ssd_kernel(xbc_ref, z_ref, dt_ref, convw_ref, convb_ref, dtb_ref, alog_ref, dskip_ref,
                normw_ref, shift_ref, headsel_ref, headdiag_ref, chansel_ref,
                o_ref, state_ref, ubuf_ref, x_ref, bc_ref, *, n_heads):
    L, G = SSM_CHUNK, SSM_GROUPS
    hist = ubuf_ref.shape[1] - L

    @pl.when(pl.program_id(1) == 0)
    def _():
        state_ref[...] = jnp.zeros_like(state_ref)
        ubuf_ref[0, 0:hist, :] = jnp.zeros((hist, ubuf_ref.shape[2]), BF16)

    causal = (lax.broadcasted_iota(jnp.int32, (L, L), 0)
              >= lax.broadcasted_iota(jnp.int32, (L, L), 1))
    gw = x_ref.shape[2] // G
    head_of_lane = lax.broadcasted_iota(jnp.int32, (L, gw), 1) // SSM_HEAD_DIM
    a = -jnp.exp(alog_ref[...])

    for ci in range(SSM_STEP_CHUNKS):
        rows = slice(ci * L, (ci + 1) * L)
        nxt = (ci + 1) % SSM_STEP_CHUNKS
        _ssd_chunk(xbc_ref.at[rows], z_ref.at[rows], dt_ref[0, :, rows], o_ref.at[rows],
                   ubuf_ref.at[ci], ubuf_ref.at[nxt], x_ref.at[ci], bc_ref.at[ci],
                   convw_ref, convb_ref, dtb_ref, a, dskip_ref, normw_ref, shift_ref,
                   headsel_ref, headdiag_ref, chansel_ref, state_ref, causal, head_of_lane,
                   n_heads=n_heads)


def _ssd_chunk(xbc_ref, z_ref, dt_raw, o_ref, ubuf_ref, ubuf_next_ref, x_ref, bc_ref,
               convw_ref, convb_ref, dtb_ref, a, dskip_ref, normw_ref, shift_ref,
               headsel_ref, headdiag_ref, chansel_ref, state_ref, causal, head_of_lane, *, n_heads):
    L, P, N, G = SSM_CHUNK, SSM_HEAD_DIM, SSM_STATE, SSM_GROUPS
    H = n_heads
    hpg = H // G
    gw = hpg * P
    slab = gw + 2 * N
    hist = ubuf_ref.shape[0] - L

    ubuf_ref[hist:hist + L, :] = xbc_ref[...]
    for g in range(G):
        cs = slice(g * slab, (g + 1) * slab)
        u = ubuf_ref[:, cs]
        shifted = _dot(shift_ref[...], u)
        acc = convb_ref[:, cs] + convw_ref[SSM_CONV - 1:SSM_CONV, cs] * u[hist:].astype(F32)
        for k in range(SSM_CONV - 1):
            acc = acc + convw_ref[k:k + 1, cs] * shifted[k * L:(k + 1) * L]
        act = _silu_of_twice(acc)
        x_ref[:, g * gw:(g + 1) * gw] = act[:, :gw]
        bc_ref[:, g * 2 * N:(g + 1) * 2 * N] = act[:, gw:].astype(BF16)
    ubuf_next_ref[0:hist, :] = ubuf_ref[L:L + hist, :]

    dt = _softplus(dt_raw + dtb_ref[...])
    acum = _cumsum_lanes(dt * a)
    last = acum[:, L - 1:L]
    wrow = dt * jnp.exp(last - acum)
    acum2 = acum * LOG2E

    neg_rows = [jnp.concatenate([-p] * H, axis=1) * headdiag_ref[...] for p in _split3_bf16(acum2)]
    neg_rows.append(jnp.zeros((H, H * L), BF16))
    seg_lhs = jnp.concatenate([_split3_t(acum2), jnp.ones((L, 4 * H), BF16)], axis=1)
    chan_lhs = jnp.concatenate([_split3_t(dt), _split3_t(wrow), _split3_t(jnp.exp(acum))], axis=0)

    def b_of(g):
        return bc_ref[:, g * 2 * N:g * 2 * N + N]

    def c_of(g):
        return bc_ref[:, g * 2 * N + N:(g + 1) * 2 * N]

    def expand(g):
        hs = slice(g * hpg * L, (g + 1) * hpg * L)
        seg_rhs = jnp.concatenate([headsel_ref[:, hs]] + [r[:, hs] for r in neg_rows], axis=0)
        return (_dot(seg_lhs, seg_rhs), _dot(chan_lhs, chansel_ref[:, g * gw:(g + 1) * gw]),
                _dot_nt(c_of(g), b_of(g)))

    expanded = expand(0)
    for g in range(G):
        sl = slice(g * gw, (g + 1) * gw)
        seg, chan, cb = expanded
        if g + 1 < G:
            expanded = expand(g + 1)
        x_g, b_g, c_g = x_ref[:, sl], b_of(g), c_of(g)
        st = state_ref[g]
        y_state = _dot(c_g, st.astype(BF16))
        xw = (x_g * chan[L:2 * L]).astype(BF16)
        state_ref[g] = (st * chan[3 * L - 1:3 * L]
                        + lax.dot_general(b_g, xw, TN_DIMS, preferred_element_type=F32))
        m_g = jnp.concatenate(
            [(jnp.exp2(jnp.where(causal, seg[:, r * L:(r + 1) * L], -jnp.inf)) * cb).astype(BF16)
             for r in range(hpg)], axis=1)
        xdt = (x_g * chan[0:L]).astype(BF16)
        xdt_diag = jnp.concatenate(
            [jnp.where(head_of_lane == r, xdt, jnp.zeros_like(xdt)) for r in range(hpg)], axis=0)
        y = _dot(m_g, xdt_diag) + y_state * chan[2 * L:3 * L] + dskip_ref[:, sl] * x_g
        y = y * _silu_of_twice(z_ref[:, sl].astype(F32))
        y = y * lax.rsqrt(jnp.mean(y * y, axis=-1, keepdims=True) + RMS_EPS)
        o_ref[:, sl] = (y * normw_ref[:, sl]).astype(BF16)


def _ssd_constants(n_heads, d_inner):
    L, H = SSM_CHUNK, n_heads
    hist = 2 * V7X_SUBLANES
    t = jnp.arange(L)[:, None]
    src = jnp.arange(hist + L)[None, :]
    shift = jnp.concatenate([(src == t + hist - d).astype(BF16)
                             for d in range(SSM_CONV - 1, 0, -1)], axis=0)
    k = jnp.arange(4 * H)[:, None]
    live = k < 3 * H
    headsel = (live & (k % H == jnp.arange(H * L)[None, :] // L)).astype(BF16)
    headdiag = (jnp.arange(H)[:, None] == jnp.arange(H * L)[None, :] // L).astype(BF16)
    chansel = (live & (k % H == jnp.arange(d_inner)[None, :] // SSM_HEAD_DIM)).astype(BF16)
    return shift, headsel, headdiag, chansel, hist


def _conv_group_major(t, d_inner):
    G, N = SSM_GROUPS, SSM_STATE
    lead = t.shape[:-1]
    x = t[..., :d_inner].reshape(lead + (G, d_inner // G))
    b = t[..., d_inner:d_inner + G * N].reshape(lead + (G, N))
    c = t[..., d_inner + G * N:].reshape(lead + (G, N))
    return jnp.concatenate([x, b, c], axis=-1).reshape(t.shape)


def _ssd(proj, dt_t, conv_w, conv_b, dt_bias, a_log, d_skip, norm_w, *, bsz, seq, d_inner, n_heads):
    conv_dim = conv_w.shape[1]
    L = SSM_CHUNK
    rows = SSM_STEP_CHUNKS * L
    assert seq % rows == 0 and conv_dim % d_inner == 0 and 4 * n_heads == L
    assert conv_dim == d_inner + 2 * SSM_GROUPS * SSM_STATE
    nc = seq // rows
    z_blk = conv_dim // d_inner
    gw = d_inner // SSM_GROUPS
    shift, headsel, headdiag, chansel, hist = _ssd_constants(n_heads, d_inner)
    tok = lambda b, c: (b * nc + c, 0)
    return pl.pallas_call(
        functools.partial(_ssd_kernel, n_heads=n_heads),
        grid=(bsz, nc),
        in_specs=[pl.BlockSpec((rows, conv_dim), tok),
                  pl.BlockSpec((rows, d_inner), lambda b, c: (b * nc + c, z_blk)),
                  pl.BlockSpec((1, n_heads, rows), lambda b, c: (b, 0, c)),
                  _resident((SSM_CONV, conv_dim)), _resident((1, conv_dim)),
                  _resident((n_heads, 1)), _resident((n_heads, 1)),
                  _resident((1, d_inner)), _resident((1, d_inner)),
                  _resident(shift.shape), _resident(headsel.shape), _resident(headdiag.shape),
                  _resident(chansel.shape)],
        out_specs=pl.BlockSpec((rows, d_inner), tok),
        out_shape=jax.ShapeDtypeStruct((bsz * seq, d_inner), BF16),
        scratch_shapes=[pltpu.VMEM((SSM_GROUPS, SSM_STATE, gw), F32),
                        pltpu.VMEM((SSM_STEP_CHUNKS, hist + L, conv_dim), BF16),
                        pltpu.VMEM((SSM_STEP_CHUNKS, L, d_inner), F32),
                        pltpu.VMEM((SSM_STEP_CHUNKS, L, conv_dim - d_inner), BF16)],
        compiler_params=_params(("parallel", "arbitrary")),
        name="ssd",
    )(proj, proj, dt_t, conv_w, conv_b.reshape(1, conv_dim),
      dt_bias.reshape(n_heads, 1), a_log.reshape(n_heads, 1),
      jnp.repeat(d_skip, SSM_HEAD_DIM).reshape(1, d_inner), norm_w.reshape(1, d_inner),
      shift, headsel, headdiag, chansel)


def _moba_kernel(qt_ref, k_ref, vt_ref, o_ref, *, nb):
    blk, D = MOBA_BLOCK, ATT_HEAD_DIM
    seq = nb * blk

    bi = lax.broadcasted_iota(jnp.int32, (nb, seq), 0)
    ti = lax.broadcasted_iota(jnp.int32, (nb, seq), 1)
    in_blk = (ti >= bi * blk) & (ti < (bi + 1) * blk)
    avg = jnp.where(in_blk, 1.0 / blk, 0.0).astype(BF16)

    key_idx = lax.broadcasted_iota(jnp.int32, (blk, blk), 0)
    qry_idx = lax.broadcasted_iota(jnp.int32, (blk, blk), 1)
    causal = key_idx <= qry_idx

    def scores(task):
        hh, i = task
        hs = slice(hh * D, (hh + 1) * D)
        return _dot(k_ref[0:(i + 1) * blk, hs], qt_ref[hh, :, i * blk:(i + 1) * blk])

    def gate_scores(hh):
        hs = slice(hh * D, (hh + 1) * D)
        k_mean = _dot(avg, k_ref[:, hs])
        gate3 = _dot(jnp.concatenate(_split3_bf16(k_mean), axis=0), qt_ref[hh])
        return gate3[0:nb] + gate3[nb:2 * nb] + gate3[2 * nb:3 * nb]

    tasks = [(hh, i) for hh in range(MOBA_STEP_HEADS) for i in range(nb)]
    ahead = 3
    s_queue = [scores(t) for t in tasks[:ahead]]
    gates = {}

    for n, (hh, i) in enumerate(tasks):
        qs = slice(i * blk, (i + 1) * blk)
        n_keys = (i + 1) * blk
        s_all = s_queue.pop(0)
        if n + ahead < len(tasks):
            s_queue.append(scores(tasks[n + ahead]))
        if hh not in gates:
            gates[hh] = gate_scores(hh)
        g_rows = [gates[hh][j:j + 1, qs] for j in range(i)]

        blocks = []
        for j in range(i + 1):
            s = s_all[j * blk:(j + 1) * blk]
            if j == i:
                s = jnp.where(causal, s, -jnp.inf)
            elif i > MOBA_TOPK:
                rank = jnp.zeros((1, blk), jnp.int32)
                for jj in range(i):
                    if jj < j:
                        rank = rank + (g_rows[jj] >= g_rows[j]).astype(jnp.int32)
                    elif jj > j:
                        rank = rank + (g_rows[jj] > g_rows[j]).astype(jnp.int32)
                s = jnp.where(rank < MOBA_TOPK, s, -jnp.inf)
            blocks.append(s)

        m = functools.reduce(jnp.maximum, [jnp.max(s, axis=0, keepdims=True) for s in blocks])
        p_all = jnp.concatenate([jnp.exp2(s - m).astype(BF16) for s in blocks], axis=0)
        acc = _dot(vt_ref[hh, :, 0:n_keys], p_all)
        o_ref[qs, hh * D:(hh + 1) * D] = (acc[0:D] * (1.0 / acc[D:D + 1])).T.astype(BF16)


def _moba(proj, q_t, v_t, *, bsz, seq, n_heads, k_col):
    d, hps = ATT_HEAD_DIM, MOBA_STEP_HEADS
    assert seq % MOBA_BLOCK == 0 and n_heads % hps == 0 and k_col % hps == 0
    nb = seq // MOBA_BLOCK
    return pl.pallas_call(
        functools.partial(_moba_kernel, nb=nb),
        grid=(bsz, n_heads // hps),
        in_specs=[pl.BlockSpec((None, hps, d, seq), lambda b, h: (b, h, 0, 0)),
                  pl.BlockSpec((seq, hps * d), lambda b, h: (b, k_col // hps + h)),
                  pl.BlockSpec((None, hps, d + ONES_ROWS, seq), lambda b, h: (b, h, 0, 0))],
        out_specs=pl.BlockSpec((seq, hps * d), lambda b, h: (b, h)),
        out_shape=jax.ShapeDtypeStruct((bsz * seq, n_heads * d), BF16),
        compiler_params=_params(("parallel", "arbitrary")),
        name="moba",
    )(q_t, proj, v_t)


def _merge_ln_kernel(ys_ref, ya_ref, gr_ref, h_ref, wso_ref, wao_ref, wo_ref, bg_ref,
                     lg_ref, lb_ref, o_ref):
    d = h_ref.shape[1]
    for rs in _sub_tiles(h_ref.shape[0]):
        y_ssm = _dot(ys_ref[rs, :], wso_ref[...])
        y_att = _dot(ya_ref[rs, :], wao_ref[...])
        gates = jax.nn.sigmoid(gr_ref[rs, :].astype(F32) + bg_ref[...])
        merged = gates[:, :d] * y_ssm + gates[:, d:] * y_att
        mix = _dot(merged.astype(BF16), wo_ref[...])
        o_ref[rs, :] = _layer_norm(DEEPNORM_ALPHA * h_ref[rs, :] + mix, lg_ref[...], lb_ref[...])


def _merge_ln(y_ssm, y_att, proj, h, w_ssm_out, w_att_out, w_o, b_gate, ln_g, ln_b, *, gate_col):
    n, d = h.shape
    d_inner, att_w = y_ssm.shape[1], y_att.shape[1]
    row = lambda i: (i, 0)
    return pl.pallas_call(
        _merge_ln_kernel,
        grid=(n // FFN_ROWS,),
        in_specs=[pl.BlockSpec((FFN_ROWS, d_inner), row),
                  pl.BlockSpec((FFN_ROWS, att_w), row),
                  pl.BlockSpec((FFN_ROWS, 2 * d), lambda i: (i, gate_col)),
                  pl.BlockSpec((FFN_ROWS, d), row),
                  _resident((d_inner, d)), _resident((att_w, d)), _resident((d, d)),
                  _resident((1, 2 * d)), _resident((1, d)), _resident((1, d))],
        out_specs=pl.BlockSpec((FFN_ROWS, d), row),
        out_shape=jax.ShapeDtypeStruct((n, d), F32),
        compiler_params=_params(("parallel",)),
        name="merge_ln",
    )(y_ssm, y_att, proj, h, w_ssm_out.astype(BF16), w_att_out.astype(BF16), w_o.astype(BF16),
      b_gate.reshape(1, 2 * d), ln_g.reshape(1, d), ln_b.reshape(1, d))


def kernel(x, ffn1_w_gate, ffn1_w_up, ffn1_w_down, ln1_g, ln1_b, w_in, conv_w, conv_b, dt_bias,
           a_log, d_skip, ssm_norm_w, w_ssm_out, w_att_out, b_gate, w_o, ln2_g, ln2_b,
           ffn2_w_gate, ffn2_w_up, ffn2_w_down, ln3_g, ln3_b):
    bsz, seq, d = x.shape
    n = bsz * seq
    d_inner = w_ssm_out.shape[0]
    conv_dim = conv_w.shape[1]
    n_ssm_heads = dt_bias.shape[0]
    att_w = w_att_out.shape[0]
    n_att_heads = att_w // ATT_HEAD_DIM

    edges = [0]
    for width in (d_inner, conv_dim, n_ssm_heads, att_w, att_w, att_w, 2 * d):
        edges.append(edges[-1] + width)
    assert edges[-1] == w_in.shape[1]
    w_z, w_xbc, w_dt, w_q, w_k, w_v, w_g = (w_in[:, lo:hi] for lo, hi in zip(edges[:-1], edges[1:]))
    w_q = w_q * (LOG2E / math.sqrt(ATT_HEAD_DIM))
    w_xbc = _conv_group_major(w_xbc, d_inner)
    w_main = jnp.concatenate([w_xbc, 0.5 * w_z, w_g, w_k], axis=1).astype(BF16)
    conv_w_half = _conv_group_major(0.5 * conv_w, d_inner)
    conv_b_half = _conv_group_major(0.5 * conv_b, d_inner)
    g_off = conv_dim + d_inner
    k_off = g_off + 2 * d
    assert k_off % ATT_HEAD_DIM == 0 and g_off % (2 * d) == 0

    h1, h1b = _ffn_ln(x.reshape(n, d), ffn1_w_gate, ffn1_w_up, ffn1_w_down, ln1_g, ln1_b,
                      with_bf16_out=True)
    proj, q_t, v_t, dt_t = _proj(h1b, w_main, w_q.T.astype(BF16), w_v.T.astype(BF16), w_dt.T.astype(BF16),
                                 bsz, seq)
    y_ssm = _ssd(proj, dt_t, conv_w_half, conv_b_half, dt_bias, a_log, d_skip, ssm_norm_w,
                 bsz=bsz, seq=seq, d_inner=d_inner, n_heads=n_ssm_heads)
    y_att = _moba(proj, q_t, v_t, bsz=bsz, seq=seq, n_heads=n_att_heads, k_col=k_off // ATT_HEAD_DIM)
    h2 = _merge_ln(y_ssm, y_att, proj, h1, w_ssm_out, w_att_out, w_o, b_gate, ln2_g, ln2_b,
                   gate_col=g_off // (2 * d))
    (out,) = _ffn_ln(h2, ffn2_w_gate, ffn2_w_up, ffn2_w_down, ln3_g, ln3_b, with_bf16_out=False)
    return out.reshape(bsz, seq, d)
```

```python
import functools
import math

import jax
import jax.numpy as jnp
from jax import lax
from jax.experimental import pallas as pl
from jax.experimental.pallas import tpu as pltpu

F32 = jnp.float32
BF16 = jnp.bfloat16

SSM_HEAD_DIM = 64
SSM_GROUPS = 8
SSM_STATE = 128
SSM_CONV = 4
SSM_CHUNK = 128
ATT_HEAD_DIM = 128
MOBA_BLOCK = 256
MOBA_TOPK = 3
DEPTH = 1
DEEPNORM_ALPHA = (2 * DEPTH) ** 0.25
LN_EPS = 1e-5
RMS_EPS = 1e-5

V7X_VMEM_BYTES = 64 * 1024 * 1024
V7X_SUBLANES = 8
VMEM_LIMIT_BYTES = V7X_VMEM_BYTES - 8 * 1024 * 1024

FFN_ROWS = 1024
FFN_SUB_ROWS = 256
FFN_CHUNK = 256
PROJ_ROWS = 512
PROJ_COLS = 1024
SSM_STEP_CHUNKS = 4
MOBA_STEP_HEADS = 4

NT_DIMS = (((1,), (1,)), ((), ()))
TN_DIMS = (((0,), (0,)), ((), ()))
LOG2E = 1.4426950408889634


def _params(semantics):
    return pltpu.CompilerParams(dimension_semantics=semantics, vmem_limit_bytes=VMEM_LIMIT_BYTES)


def _resident(shape):
    nd = len(shape)
    return pl.BlockSpec(shape, lambda *_: (0,) * nd, pipeline_mode=pl.Buffered(1))


def _dot(a, b):
    return jnp.dot(a, b, preferred_element_type=F32)


def _dot_nt(a, b):
    return lax.dot_general(a, b, NT_DIMS, preferred_element_type=F32)


def _silu(v):
    return v * jax.nn.sigmoid(v)


def _silu_of_twice(h):
    return h * (1.0 + jnp.tanh(h))


def _sub_tiles(rows):
    assert rows % FFN_SUB_ROWS == 0
    return [slice(r, r + FFN_SUB_ROWS) for r in range(0, rows, FFN_SUB_ROWS)]


def _layer_norm(r, g, b):
    mu = jnp.mean(r, axis=-1, keepdims=True)
    d = r - mu
    var = jnp.mean(d * d, axis=-1, keepdims=True)
    return d * lax.rsqrt(var + LN_EPS) * g + b


def _ffn_ln_kernel(x_ref, wg_ref, wu_ref, wd_ref, g_ref, b_ref, *refs, d_ff, with_bf16_out):
    if with_bf16_out:
        o_ref, ob_ref, act_ref = refs
    else:
        o_ref, act_ref = refs
    for rs in _sub_tiles(x_ref.shape[0]):
        x = x_ref[rs, :]
        xb = x.astype(BF16)
        for c in range(d_ff // FFN_CHUNK):
            sl = slice(c * FFN_CHUNK, (c + 1) * FFN_CHUNK)
            gate = _dot(xb, wg_ref[:, sl])
            up = _dot(xb, wu_ref[:, sl])
            act_ref[rs, sl] = (_silu(gate) * up).astype(BF16)
        y = _dot(act_ref[rs, :], wd_ref[...])
        out = _layer_norm(DEEPNORM_ALPHA * x + 0.5 * y, g_ref[...], b_ref[...])
        o_ref[rs, :] = out
        if with_bf16_out:
            ob_ref[rs, :] = out.astype(BF16)


def _ffn_ln(x, wg, wu, wd, g, b, *, with_bf16_out):
    n, d = x.shape
    d_ff = wg.shape[1]
    assert n % FFN_ROWS == 0 and d_ff % FFN_CHUNK == 0
    row = lambda i: (i, 0)
    out_shape = [jax.ShapeDtypeStruct((n, d), F32)]
    out_specs = [pl.BlockSpec((FFN_ROWS, d), row)]
    if with_bf16_out:
        out_shape.append(jax.ShapeDtypeStruct((n, d), BF16))
        out_specs.append(pl.BlockSpec((FFN_ROWS, d), row))
    return pl.pallas_call(
        functools.partial(_ffn_ln_kernel, d_ff=d_ff, with_bf16_out=with_bf16_out),
        grid=(n // FFN_ROWS,),
        in_specs=[pl.BlockSpec((FFN_ROWS, d), row),
                  _resident((d, d_ff)), _resident((d, d_ff)), _resident((d_ff, d)),
                  _resident((1, d)), _resident((1, d))],
        out_specs=out_specs,
        out_shape=out_shape,
        scratch_shapes=[pltpu.VMEM((FFN_ROWS, d_ff), BF16)],
        compiler_params=_params(("parallel",)),
        name="ffn_ln",
    )(x, wg.astype(BF16), wu.astype(BF16), wd.astype(BF16), g.reshape(1, d), b.reshape(1, d))


ONES_ROWS = 2 * V7X_SUBLANES


def _proj_kernel(x_ref, w_ref, wq_ref, wv_ref, wdt_ref, o_ref, qt_ref, vt_ref, dtt_ref):
    x = x_ref[...]
    for j in range(w_ref.shape[1] // PROJ_COLS):
        cs = slice(j * PROJ_COLS, (j + 1) * PROJ_COLS)
        o_ref[:, cs] = _dot(x, w_ref[:, cs]).astype(BF16)
    n_heads, rows, cols = vt_ref.shape[1:]
    hd = rows - ONES_ROWS
    q_t = _dot_nt(wq_ref[...], x).astype(BF16)
    v_t = _dot_nt(wv_ref[...], x).astype(BF16)
    for h in range(n_heads):
        qt_ref[0, h] = q_t[h * hd:(h + 1) * hd]
        vt_ref[0, h, 0:hd, :] = v_t[h * hd:(h + 1) * hd]
        vt_ref[0, h, hd:rows, :] = jnp.ones((ONES_ROWS, cols), BF16)
    dtt_ref[0] = _dot_nt(wdt_ref[...], x)


def _proj(xb, w, w_qt, w_vt, w_dtt, bsz, seq):
    n, d = xb.shape
    cols = w.shape[1]
    cv, cdt = w_vt.shape[0], w_dtt.shape[0]
    assert seq % PROJ_ROWS == 0 and cols % PROJ_COLS == 0 and cv % ATT_HEAD_DIM == 0 and w_qt.shape == w_vt.shape
    spb = seq // PROJ_ROWS
    n_heads = cv // ATT_HEAD_DIM
    rows = ATT_HEAD_DIM + ONES_ROWS
    return pl.pallas_call(
        _proj_kernel,
        grid=(bsz, spb),
        in_specs=[pl.BlockSpec((PROJ_ROWS, d), lambda b, s: (b * spb + s, 0)),
                  _resident((d, cols)), _resident((cv, d)), _resident((cv, d)), _resident((cdt, d))],
        out_specs=[pl.BlockSpec((PROJ_ROWS, cols), lambda b, s: (b * spb + s, 0)),
                   pl.BlockSpec((1, n_heads, ATT_HEAD_DIM, PROJ_ROWS), lambda b, s: (b, 0, 0, s)),
                   pl.BlockSpec((1, n_heads, rows, PROJ_ROWS), lambda b, s: (b, 0, 0, s)),
                   pl.BlockSpec((1, cdt, PROJ_ROWS), lambda b, s: (b, 0, s))],
        out_shape=[jax.ShapeDtypeStruct((n, cols), BF16),
                   jax.ShapeDtypeStruct((bsz, n_heads, ATT_HEAD_DIM, seq), BF16),
                   jax.ShapeDtypeStruct((bsz, n_heads, rows, seq), BF16),
                   jax.ShapeDtypeStruct((bsz, cdt, seq), F32)],
        compiler_params=_params(("parallel", "arbitrary")),
        name="in_proj",
    )(xb, w, w_qt, w_vt, w_dtt)


def _cumsum_lanes(v):
    n = v.shape[-1]
    lane = lax.broadcasted_iota(jnp.int32, v.shape, v.ndim - 1)
    shift = 1
    while shift < n:
        v = v + jnp.where(lane >= shift, pltpu.roll(v, shift, v.ndim - 1), 0.0)
        shift *= 2
    return v


def _softplus(v):
    return jnp.maximum(v, 0.0) + jnp.log1p(jnp.exp(-jnp.abs(v)))


def _split3_bf16(v):
    hi = v.astype(BF16)
    r1 = v - hi.astype(F32)
    mid = r1.astype(BF16)
    lo = (r1 - mid.astype(F32)).astype(BF16)
    return hi, mid, lo


def _split3_t(v):
    parts = [p.astype(F32) for p in _split3_bf16(v)] + [jnp.zeros_like(v)]
    return jnp.concatenate(parts, axis=0).T.astype(BF16)


def _ssd_kernel(xbc_ref, z_ref, dt_ref, convw_ref, convb_ref, dtb_ref, alog_ref, dskip_ref,
                normw_ref, shift_ref, headsel_ref, headdiag_ref, chansel_ref,
                o_ref, state_ref, ubuf_ref, x_ref, bc_ref, *, n_heads):
    L, G = SSM_CHUNK, SSM_GROUPS
    hist = ubuf_ref.shape[1] - L

    @pl.when(pl.program_id(1) == 0)
    def _():
        state_ref[...] = jnp.zeros_like(state_ref)
        ubuf_ref[0, 0:hist, :] = jnp.zeros((hist, ubuf_ref.shape[2]), BF16)

    causal = (lax.broadcasted_iota(jnp.int32, (L, L), 0)
              >= lax.broadcasted_iota(jnp.int32, (L, L), 1))
    gw = x_ref.shape[2] // G
    head_of_lane = lax.broadcasted_iota(jnp.int32, (L, gw), 1) // SSM_HEAD_DIM
    a = -jnp.exp(alog_ref[...])

    for ci in range(SSM_STEP_CHUNKS):
        rows = slice(ci * L, (ci + 1) * L)
        nxt = (ci + 1) % SSM_STEP_CHUNKS
        _ssd_chunk(xbc_ref.at[rows], z_ref.at[rows], dt_ref[0, :, rows], o_ref.at[rows],
                   ubuf_ref.at[ci], ubuf_ref.at[nxt], x_ref.at[ci], bc_ref.at[ci],
                   convw_ref, convb_ref, dtb_ref, a, dskip_ref, normw_ref, shift_ref,
                   headsel_ref, headdiag_ref, chansel_ref, state_ref, causal, head_of_lane,
                   n_heads=n_heads)


def _ssd_chunk(xbc_ref, z_ref, dt_raw, o_ref, ubuf_ref, ubuf_next_ref, x_ref, bc_ref,
               convw_ref, convb_ref, dtb_ref, a, dskip_ref, normw_ref, shift_ref,
               headsel_ref, headdiag_ref, chansel_ref, state_ref, causal, head_of_lane, *, n_heads):
    L, P, N, G = SSM_CHUNK, SSM_HEAD_DIM, SSM_STATE, SSM_GROUPS
    H = n_heads
    hpg = H // G
    gw = hpg * P
    slab = gw + 2 * N
    hist = ubuf_ref.shape[0] - L

    ubuf_ref[hist:hist + L, :] = xbc_ref[...]
    for g in range(G):
        cs = slice(g * slab, (g + 1) * slab)
        u = ubuf_ref[:, cs]
        shifted = _dot(shift_ref[...], u)
        acc = convb_ref[:, cs] + convw_ref[SSM_CONV - 1:SSM_CONV, cs] * u[hist:].astype(F32)
        for k in range(SSM_CONV - 1):
            acc = acc + convw_ref[k:k + 1, cs] * shifted[k * L:(k + 1) * L]
        act = _silu_of_twice(acc)
        x_ref[:, g * gw:(g + 1) * gw] = act[:, :gw]
        bc_ref[:, g * 2 * N:(g + 1) * 2 * N] = act[:, gw:].astype(BF16)
    ubuf_next_ref[0:hist, :] = ubuf_ref[L:L + hist, :]

    dt = _softplus(dt_raw + dtb_ref[...])
    acum = _cumsum_lanes(dt * a)
    last = acum[:, L - 1:L]
    wrow = dt * jnp.exp(last - acum)
    acum2 = acum * LOG2E

    neg_rows = [jnp.concatenate([-p] * H, axis=1) * headdiag_ref[...] for p in _split3_bf16(acum2)]
    neg_rows.append(jnp.zeros((H, H * L), BF16))
    seg_lhs = jnp.concatenate([_split3_t(acum2), jnp.ones((L, 4 * H), BF16)], axis=1)
    chan_lhs = jnp.concatenate([_split3_t(dt), _split3_t(wrow), _split3_t(jnp.exp(acum))], axis=0)

    def b_of(g):
        return bc_ref[:, g * 2 * N:g * 2 * N + N]

    def c_of(g):
        return bc_ref[:, g * 2 * N + N:(g + 1) * 2 * N]

    def expand(g):
        hs = slice(g * hpg * L, (g + 1) * hpg * L)
        seg_rhs = jnp.concatenate([headsel_ref[:, hs]] + [r[:, hs] for r in neg_rows], axis=0)
        return (_dot(seg_lhs, seg_rhs), _dot(chan_lhs, chansel_ref[:, g * gw:(g + 1) * gw]),
                _dot_nt(c_of(g), b_of(g)))

    expanded = expand(0)
    for g in range(G):
        sl = slice(g * gw, (g + 1) * gw)
        seg, chan, cb = expanded
        if g + 1 < G:
            expanded = expand(g + 1)
        x_g, b_g, c_g = x_ref[:, sl], b_of(g), c_of(g)
        st = state_ref[g]
        y_state = _dot(c_g, st.astype(BF16))
        xw = (x_g * chan[L:2 * L]).astype(BF16)
        state_ref[g] = (st * chan[3 * L - 1:3 * L]
                        + lax.dot_general(b_g, xw, TN_DIMS, preferred_element_type=F32))
        m_g = jnp.concatenate(
            [(jnp.exp2(jnp.where(causal, seg[:, r * L:(r + 1) * L], -jnp.inf)) * cb).astype(BF16)
             for r in range(hpg)], axis=1)
        xdt = (x_g * chan[0:L]).astype(BF16)
        xdt_diag = jnp.concatenate(
            [jnp.where(head_of_lane == r, xdt, jnp.zeros_like(xdt)) for r in range(hpg)], axis=0)
        y = _dot(m_g, xdt_diag) + y_state * chan[2 * L:3 * L] + dskip_ref[:, sl] * x_g
        y = y * _silu_of_twice(z_ref[:, sl].astype(F32))
        y = y * lax.rsqrt(jnp.mean(y * y, axis=-1, keepdims=True) + RMS_EPS)
        o_ref[:, sl] = (y * normw_ref[:, sl]).astype(BF16)


def _ssd_constants(n_heads, d_inner):
    L, H = SSM_CHUNK, n_heads
    hist = 2 * V7X_SUBLANES
    t = jnp.arange(L)[:, None]
    src = jnp.arange(hist + L)[None, :]
    shift = jnp.concatenate([(src == t + hist - d).astype(BF16)
                             for d in range(SSM_CONV - 1, 0, -1)], axis=0)
    k = jnp.arange(4 * H)[:, None]
    live = k < 3 * H
    headsel = (live & (k % H == jnp.arange(H * L)[None, :] // L)).astype(BF16)
    headdiag = (jnp.arange(H)[:, None] == jnp.arange(H * L)[None, :] // L).astype(BF16)
    chansel = (live & (k % H == jnp.arange(d_inner)[None, :] // SSM_HEAD_DIM)).astype(BF16)
    return shift, headsel, headdiag, chansel, hist


def _conv_group_major(t, d_inner):
    G, N = SSM_GROUPS, SSM_STATE
    lead = t.shape[:-1]
    x = t[..., :d_inner].reshape(lead + (G, d_inner // G))
    b = t[..., d_inner:d_inner + G * N].reshape(lead + (G, N))
    c = t[..., d_inner + G * N:].reshape(lead + (G, N))
    return jnp.concatenate([x, b, c], axis=-1).reshape(t.shape)


def _ssd(proj, dt_t, conv_w, conv_b, dt_bias, a_log, d_skip, norm_w, *, bsz, seq, d_inner, n_heads):
    conv_dim = conv_w.shape[1]
    L = SSM_CHUNK
    rows = SSM_STEP_CHUNKS * L
    assert seq % rows == 0 and conv_dim % d_inner == 0 and 4 * n_heads == L
    assert conv_dim == d_inner + 2 * SSM_GROUPS * SSM_STATE
    nc = seq // rows
    z_blk = conv_dim // d_inner
    gw = d_inner // SSM_GROUPS
    shift, headsel, headdiag, chansel, hist = _ssd_constants(n_heads, d_inner)
    tok = lambda b, c: (b * nc + c, 0)
    return pl.pallas_call(
        functools.partial(_ssd_kernel, n_heads=n_heads),
        grid=(bsz, nc),
        in_specs=[pl.BlockSpec((rows, conv_dim), tok),
                  pl.BlockSpec((rows, d_inner), lambda b, c: (b * nc + c, z_blk)),
                  pl.BlockSpec((1, n_heads, rows), lambda b, c: (b, 0, c)),
                  _resident((SSM_CONV, conv_dim)), _resident((1, conv_dim)),
                  _resident((n_heads, 1)), _resident((n_heads, 1)),
                  _resident((1, d_inner)), _resident((1, d_inner)),
                  _resident(shift.shape), _resident(headsel.shape), _resident(headdiag.shape),
                  _resident(chansel.shape)],
        out_specs=pl.BlockSpec((rows, d_inner), tok),
        out_shape=jax.ShapeDtypeStruct((bsz * seq, d_inner), BF16),
        scratch_shapes=[pltpu.VMEM((SSM_GROUPS, SSM_STATE, gw), F32),
                        pltpu.VMEM((SSM_STEP_CHUNKS, hist + L, conv_dim), BF16),
                        pltpu.VMEM((SSM_STEP_CHUNKS, L, d_inner), F32),
                        pltpu.VMEM((SSM_STEP_CHUNKS, L, conv_dim - d_inner), BF16)],
        compiler_params=_params(("parallel", "arbitrary")),
        name="ssd",
    )(proj, proj, dt_t, conv_w, conv_b.reshape(1, conv_dim),
      dt_bias.reshape(n_heads, 1), a_log.reshape(n_heads, 1),
      jnp.repeat(d_skip, SSM_HEAD_DIM).reshape(1, d_inner), norm_w.reshape(1, d_inner),
      shift, headsel, headdiag, chansel)


def _moba_kernel(qt_ref, k_ref, vt_ref, o_ref, *, nb):
    blk, D = MOBA_BLOCK, ATT_HEAD_DIM
    seq = nb * blk

    bi = lax.broadcasted_iota(jnp.int32, (nb, seq), 0)
    ti = lax.broadcasted_iota(jnp.int32, (nb, seq), 1)
    in_blk = (ti >= bi * blk) & (ti < (bi + 1) * blk)
    avg = jnp.where(in_blk, 1.0 / blk, 0.0).astype(BF16)

    key_idx = lax.broadcasted_iota(jnp.int32, (blk, blk), 0)
    qry_idx = lax.broadcasted_iota(jnp.int32, (blk, blk), 1)
    causal = key_idx <= qry_idx

    def scores(task):
        hh, i = task
        hs = slice(hh * D, (hh + 1) * D)
        return _dot(k_ref[0:(i + 1) * blk, hs], qt_ref[hh, :, i * blk:(i + 1) * blk])

    def gate_scores(hh):
        hs = slice(hh * D, (hh + 1) * D)
        k_mean = _dot(avg, k_ref[:, hs])
        gate3 = _dot(jnp.concatenate(_split3_bf16(k_mean), axis=0), qt_ref[hh])
        return gate3[0:nb] + gate3[nb:2 * nb] + gate3[2 * nb:3 * nb]

    tasks = [(hh, i) for hh in range(MOBA_STEP_HEADS) for i in range(nb)]
    ahead = 3
    s_queue = [scores(t) for t in tasks[:ahead]]
    gates = {}

    for n, (hh, i) in enumerate(tasks):
        qs = slice(i * blk, (i + 1) * blk)
        n_keys = (i + 1) * blk
        s_all = s_queue.pop(0)
        if n + ahead < len(tasks):
            s_queue.append(scores(tasks[n + ahead]))
        if hh not in gates:
            gates[hh] = gate_scores(hh)
        g_rows = [gates[hh][j:j + 1, qs] for j in range(i)]

        blocks = []
        for j in range(i + 1):
            s = s_all[j * blk:(j + 1) * blk]
            if j == i:
                s = jnp.where(causal, s, -jnp.inf)
            elif i > MOBA_TOPK:
                rank = jnp.zeros((1, blk), jnp.int32)
                for jj in range(i):
                    if jj < j:
                        rank = rank + (g_rows[jj] >= g_rows[j]).astype(jnp.int32)
                    elif jj > j:
                        rank = rank + (g_rows[jj] > g_rows[j]).astype(jnp.int32)
                s = jnp.where(rank < MOBA_TOPK, s, -jnp.inf)
            blocks.append(s)

        m = functools.reduce(jnp.maximum, [jnp.max(s, axis=0, keepdims=True) for s in blocks])
        p_all = jnp.concatenate([jnp.exp2(s - m).astype(BF16) for s in blocks], axis=0)
        acc = _dot(vt_ref[hh, :, 0:n_keys], p_all)
        o_ref[qs, hh * D:(hh + 1) * D] = (acc[0:D] * (1.0 / acc[D:D + 1])).T.astype(BF16)


def _moba(proj, q_t, v_t, *, bsz, seq, n_heads, k_col):
    d, hps = ATT_HEAD_DIM, MOBA_STEP_HEADS
    assert seq % MOBA_BLOCK == 0 and n_heads % hps == 0 and k_col % hps == 0
    nb = seq // MOBA_BLOCK
    return pl.pallas_call(
        functools.partial(_moba_kernel, nb=nb),
        grid=(bsz, n_heads // hps),
        in_specs=[pl.BlockSpec((None, hps, d, seq), lambda b, h: (b, h, 0, 0)),
                  pl.BlockSpec((seq, hps * d), lambda b, h: (b, k_col // hps + h)),
                  pl.BlockSpec((None, hps, d + ONES_ROWS, seq), lambda b, h: (b, h, 0, 0))],
        out_specs=pl.BlockSpec((seq, hps * d), lambda b, h: (b, h)),
        out_shape=jax.ShapeDtypeStruct((bsz * seq, n_heads * d), BF16),
        compiler_params=_params(("parallel", "arbitrary")),
        name="moba",
    )(q_t, proj, v_t)


def _merge_ln_kernel(ys_ref, ya_ref, gr_ref, h_ref, wso_ref, wao_ref, wo_ref, bg_ref,
                     lg_ref, lb_ref, o_ref):
    d = h_ref.shape[1]
    for rs in _sub_tiles(h_ref.shape[0]):
        y_ssm = _dot(ys_ref[rs, :], wso_ref[...])
        y_att = _dot(ya_ref[rs, :], wao_ref[...])
        gates = jax.nn.sigmoid(gr_ref[rs, :].astype(F32) + bg_ref[...])
        merged = gates[:, :d] * y_ssm + gates[:, d:] * y_att
        mix = _dot(merged.astype(BF16), wo_ref[...])
        o_ref[rs, :] = _layer_norm(DEEPNORM_ALPHA * h_ref[rs, :] + mix, lg_ref[...], lb_ref[...])


def _merge_ln(y_ssm, y_att, proj, h, w_ssm_out, w_att_out, w_o, b_gate, ln_g, ln_b, *, gate_col):
    n, d = h.shape
    d_inner, att_w = y_ssm.shape[1], y_att.shape[1]
    row = lambda i: (i, 0)
    return pl.pallas_call(
        _merge_ln_kernel,
        grid=(n // FFN_ROWS,),
        in_specs=[pl.BlockSpec((FFN_ROWS, d_inner), row),
                  pl.BlockSpec((FFN_ROWS, att_w), row),
                  pl.BlockSpec((FFN_ROWS, 2 * d), lambda i: (i, gate_col)),
                  pl.BlockSpec((FFN_ROWS, d), row),
                  _resident((d_inner, d)), _resident((att_w, d)), _resident((d, d)),
                  _resident((1, 2 * d)), _resident((1, d)), _resident((1, d))],
        out_specs=pl.BlockSpec((FFN_ROWS, d), row),
        out_shape=jax.ShapeDtypeStruct((n, d), F32),
        compiler_params=_params(("parallel",)),
        name="merge_ln",
    )(y_ssm, y_att, proj, h, w_ssm_out.astype(BF16), w_att_out.astype(BF16), w_o.astype(BF16),
      b_gate.reshape(1, 2 * d), ln_g.reshape(1, d), ln_b.reshape(1, d))


def kernel(x, ffn1_w_gate, ffn1_w_up, ffn1_w_down, ln1_g, ln1_b, w_in, conv_w, conv_b, dt_bias,
           a_log, d_skip, ssm_norm_w, w_ssm_out, w_att_out, b_gate, w_o, ln2_g, ln2_b,
           ffn2_w_gate, ffn2_w_up, ffn2_w_down, ln3_g, ln3_b):
    bsz, seq, d = x.shape
    n = bsz * seq
    d_inner = w_ssm_out.shape[0]
    conv_dim = conv_w.shape[1]
    n_ssm_heads = dt_bias.shape[0]
    att_w = w_att_out.shape[0]
    n_att_heads = att_w // ATT_HEAD_DIM

    edges = [0]
    for width in (d_inner, conv_dim, n_ssm_heads, att_w, att_w, att_w, 2 * d):
        edges.append(edges[-1] + width)
    assert edges[-1] == w_in.shape[1]
    w_z, w_xbc, w_dt, w_q, w_k, w_v, w_g = (w_in[:, lo:hi] for lo, hi in zip(edges[:-1], edges[1:]))
    w_q = w_q * (LOG2E / math.sqrt(ATT_HEAD_DIM))
    w_xbc = _conv_group_major(w_xbc, d_inner)
    w_main = jnp.concatenate([w_xbc, 0.5 * w_z, w_g, w_k], axis=1).astype(BF16)
    conv_w_half = _conv_group_major(0.5 * conv_w, d_inner)
    conv_b_half = _conv_group_major(0.5 * conv_b, d_inner)
    g_off = conv_dim + d_inner
    k_off = g_off + 2 * d
    assert k_off % ATT_HEAD_DIM == 0 and g_off % (2 * d) == 0

    h1, h1b = _ffn_ln(x.reshape(n, d), ffn1_w_gate, ffn1_w_up, ffn1_w_down, ln1_g, ln1_b,
                      with_bf16_out=True)
    proj, q_t, v_t, dt_t = _proj(h1b, w_main, w_q.T.astype(BF16), w_v.T.astype(BF16), w_dt.T.astype(BF16),
                                 bsz, seq)
    y_ssm = _ssd(proj, dt_t, conv_w_half, conv_b_half, dt_bias, a_log, d_skip, ssm_norm_w,
                 bsz=bsz, seq=seq, d_inner=d_inner, n_heads=n_ssm_heads)
    y_att = _moba(proj, q_t, v_t, bsz=bsz, seq=seq, n_heads=n_att_heads, k_col=k_off // ATT_HEAD_DIM)
    h2 = _merge_ln(y_ssm, y_att, proj, h1, w_ssm_out, w_att_out, w_o, b_gate, ln2_g, ln2_b,
                   gate_col=g_off // (2 * d))
    (out,) = _ffn_ln(h2, ffn2_w_gate, ffn2_w_up, ffn2_w_down, ln3_g, ln3_b, with_bf16_out=False)
    return out.reshape(bsz, seq, d)
```

```python
import functools
import math

import jax
import jax.numpy as jnp
from jax import lax
from jax.experimental import pallas as pl
from jax.experimental.pallas import tpu as pltpu

F32 = jnp.float32
BF16 = jnp.bfloat16

SSM_HEAD_DIM = 64
SSM_GROUPS = 8
SSM_STATE = 128
SSM_CONV = 4
SSM_CHUNK = 128
ATT_HEAD_DIM = 128
MOBA_BLOCK = 256
MOBA_TOPK = 3
DEPTH = 1
DEEPNORM_ALPHA = (2 * DEPTH) ** 0.25
LN_EPS = 1e-5
RMS_EPS = 1e-5

V7X_VMEM_BYTES = 64 * 1024 * 1024
V7X_SUBLANES = 8
VMEM_LIMIT_BYTES = V7X_VMEM_BYTES - 8 * 1024 * 1024

FFN_ROWS = 1024
FFN_SUB_ROWS = 256
FFN_CHUNK = 256
PROJ_ROWS = 512
PROJ_COLS = 1024
SSM_STEP_CHUNKS = 4
MOBA_STEP_HEADS = 4
MOBA_SCORES_AHEAD = 2

NT_DIMS = (((1,), (1,)), ((), ()))
TN_DIMS = (((0,), (0,)), ((), ()))
LOG2E = 1.4426950408889634


def _params(semantics):
    return pltpu.CompilerParams(dimension_semantics=semantics, vmem_limit_bytes=VMEM_LIMIT_BYTES)


def _resident(shape):
    nd = len(shape)
    return pl.BlockSpec(shape, lambda *_: (0,) * nd, pipeline_mode=pl.Buffered(1))


def _dot(a, b):
    return jnp.dot(a, b, preferred_element_type=F32)


def _dot_nt(a, b):
    return lax.dot_general(a, b, NT_DIMS, preferred_element_type=F32)


def _silu(v):
    return v * jax.nn.sigmoid(v)


def _silu_of_twice(h):
    return h * (1.0 + jnp.tanh(h))


def _sub_tiles(rows):
    assert rows % FFN_SUB_ROWS == 0
    return [slice(r, r + FFN_SUB_ROWS) for r in range(0, rows, FFN_SUB_ROWS)]


def _layer_norm(r, g, b):
    mu = jnp.mean(r, axis=-1, keepdims=True)
    d = r - mu
    var = jnp.mean(d * d, axis=-1, keepdims=True)
    return d * lax.rsqrt(var + LN_EPS) * g + b


def _ffn_ln_kernel(x_ref, wg_ref, wu_ref, wd_ref, g_ref, b_ref, *refs, d_ff, with_bf16_out):
    if with_bf16_out:
        o_ref, ob_ref, act_ref = refs
    else:
        o_ref, act_ref = refs
    for rs in _sub_tiles(x_ref.shape[0]):
        x = x_ref[rs, :]
        xb = x.astype(BF16)
        for c in range(d_ff // FFN_CHUNK):
            sl = slice(c * FFN_CHUNK, (c + 1) * FFN_CHUNK)
            gate = _dot(xb, wg_ref[:, sl])
            up = _dot(xb, wu_ref[:, sl])
            act_ref[rs, sl] = (_silu(gate) * up).astype(BF16)
        y = _dot(act_ref[rs, :], wd_ref[...])
        out = _layer_norm(DEEPNORM_ALPHA * x + 0.5 * y, g_ref[...], b_ref[...])
        o_ref[rs, :] = out
        if with_bf16_out:
            ob_ref[rs, :] = out.astype(BF16)


def _ffn_ln(x, wg, wu, wd, g, b, *, with_bf16_out):
    n, d = x.shape
    d_ff = wg.shape[1]
    assert n % FFN_ROWS == 0 and d_ff % FFN_CHUNK == 0
    row = lambda i: (i, 0)
    out_shape = [jax.ShapeDtypeStruct((n, d), F32)]
    out_specs = [pl.BlockSpec((FFN_ROWS, d), row)]
    if with_bf16_out:
        out_shape.append(jax.ShapeDtypeStruct((n, d), BF16))
        out_specs.append(pl.BlockSpec((FFN_ROWS, d), row))
    return pl.pallas_call(
        functools.partial(_ffn_ln_kernel, d_ff=d_ff, with_bf16_out=with_bf16_out),
        grid=(n // FFN_ROWS,),
        in_specs=[pl.BlockSpec((FFN_ROWS, d), row),
                  _resident((d, d_ff)), _resident((d, d_ff)), _resident((d_ff, d)),
                  _resident((1, d)), _resident((1, d))],
        out_specs=out_specs,
        out_shape=out_shape,
        scratch_shapes=[pltpu.VMEM((FFN_ROWS, d_ff), BF16)],
        compiler_params=_params(("parallel",)),
        name="ffn_ln",
    )(x, wg.astype(BF16), wu.astype(BF16), wd.astype(BF16), g.reshape(1, d), b.reshape(1, d))


ONES_ROWS = 2 * V7X_SUBLANES


def _proj_kernel(x_ref, w_ref, wk_ref, wq_ref, wv_ref, wdt_ref, o_ref, kh_ref, qt_ref, vt_ref, dtt_ref):
    x = x_ref[...]
    for j in range(w_ref.shape[1] // PROJ_COLS):
        cs = slice(j * PROJ_COLS, (j + 1) * PROJ_COLS)
        o_ref[:, cs] = _dot(x, w_ref[:, cs]).astype(BF16)
    n_heads, rows, cols = vt_ref.shape[1:]
    hd = rows - ONES_ROWS
    k = _dot(x, wk_ref[...]).astype(BF16)
    for h in range(n_heads):
        kh_ref[0, h] = k[:, h * hd:(h + 1) * hd]
    q_t = _dot_nt(wq_ref[...], x).astype(BF16)
    v_t = _dot_nt(wv_ref[...], x).astype(BF16)
    for h in range(n_heads):
        qt_ref[0, h] = q_t[h * hd:(h + 1) * hd]
        vt_ref[0, h, 0:hd, :] = v_t[h * hd:(h + 1) * hd]
        vt_ref[0, h, hd:rows, :] = jnp.ones((ONES_ROWS, cols), BF16)
    dtt_ref[0] = _dot_nt(wdt_ref[...], x)


def _proj(xb, w, w_k, w_qt, w_vt, w_dtt, bsz, seq):
    n, d = xb.shape
    cols = w.shape[1]
    cv, cdt = w_vt.shape[0], w_dtt.shape[0]
    assert seq % PROJ_ROWS == 0 and cols % PROJ_COLS == 0 and cv % ATT_HEAD_DIM == 0 and w_qt.shape == w_vt.shape
    spb = seq // PROJ_ROWS
    n_heads = cv // ATT_HEAD_DIM
    rows = ATT_HEAD_DIM + ONES_ROWS
    return pl.pallas_call(
        _proj_kernel,
        grid=(bsz, spb),
        in_specs=[pl.BlockSpec((PROJ_ROWS, d), lambda b, s: (b * spb + s, 0)),
                  _resident((d, cols)), _resident((d, cv)), _resident((cv, d)), _resident((cv, d)),
                  _resident((cdt, d))],
        out_specs=[pl.BlockSpec((PROJ_ROWS, cols), lambda b, s: (b * spb + s, 0)),
                   pl.BlockSpec((1, n_heads, PROJ_ROWS, ATT_HEAD_DIM), lambda b, s: (b, 0, s, 0)),
                   pl.BlockSpec((1, n_heads, ATT_HEAD_DIM, PROJ_ROWS), lambda b, s: (b, 0, 0, s)),
                   pl.BlockSpec((1, n_heads, rows, PROJ_ROWS), lambda b, s: (b, 0, 0, s)),
                   pl.BlockSpec((1, cdt, PROJ_ROWS), lambda b, s: (b, 0, s))],
        out_shape=[jax.ShapeDtypeStruct((n, cols), BF16),
                   jax.ShapeDtypeStruct((bsz, n_heads, seq, ATT_HEAD_DIM), BF16),
                   jax.ShapeDtypeStruct((bsz, n_heads, ATT_HEAD_DIM, seq), BF16),
                   jax.ShapeDtypeStruct((bsz, n_heads, rows, seq), BF16),
                   jax.ShapeDtypeStruct((bsz, cdt, seq), F32)],
        compiler_params=_params(("parallel", "arbitrary")),
        name="in_proj",
    )(xb, w, w_k, w_qt, w_vt, w_dtt)


def _cumsum_lanes(v):
    n = v.shape[-1]
    lane = lax.broadcasted_iota(jnp.int32, v.shape, v.ndim - 1)
    shift = 1
    while shift < n:
        v = v + jnp.where(lane >= shift, pltpu.roll(v, shift, v.ndim - 1), 0.0)
        shift *= 2
    return v


def _softplus(v):
    return jnp.maximum(v, 0.0) + jnp.log1p(jnp.exp(-jnp.abs(v)))


def _split3_bf16(v):
    hi = v.astype(BF16)
    r1 = v - hi.astype(F32)
    mid = r1.astype(BF16)
    lo = (r1 - mid.astype(F32)).astype(BF16)
    return hi, mid, lo


def _split3_t(v):
    parts = [p.astype(F32) for p in _split3_bf16(v)] + [jnp.zeros_like(v)]
    return jnp.concatenate(parts, axis=0).T.astype(BF16)


def _ssd_kernel(xbc_ref, z_ref, dt_ref, convw_ref, convb_ref, dtb_ref, alog_ref, dskip_ref,
                normw_ref, shift_ref, headsel_ref, headdiag_ref, chansel_ref,
                o_ref, state_ref, ubuf_ref, x_ref, bc_ref, *, n_heads):
    L, G = SSM_CHUNK, SSM_GROUPS
    hist = ubuf_ref.shape[1] - L

    @pl.when(pl.program_id(1) == 0)
    def _():
        state_ref[...] = jnp.zeros_like(state_ref)
        ubuf_ref[0, 0:hist, :] = jnp.zeros((hist, ubuf_ref.shape[2]), BF16)

    causal = (lax.broadcasted_iota(jnp.int32, (L, L), 0)
              >= lax.broadcasted_iota(jnp.int32, (L, L), 1))
    gw = x_ref.shape[2] // G
    head_of_lane = lax.broadcasted_iota(jnp.int32, (L, gw), 1) // SSM_HEAD_DIM
    a = -jnp.exp(alog_ref[...])

    for ci in range(SSM_STEP_CHUNKS):
        rows = slice(ci * L, (ci + 1) * L)
        nxt = (ci + 1) % SSM_STEP_CHUNKS
        _ssd_chunk(xbc_ref.at[rows], z_ref.at[rows], dt_ref[0, :, rows], o_ref.at[rows],
                   ubuf_ref.at[ci], ubuf_ref.at[nxt], x_ref.at[ci], bc_ref.at[ci],
                   convw_ref, convb_ref, dtb_ref, a, dskip_ref, normw_ref, shift_ref,
                   headsel_ref, headdiag_ref, chansel_ref, state_ref, causal, head_of_lane,
                   n_heads=n_heads)


def _ssd_chunk(xbc_ref, z_ref, dt_raw, o_ref, ubuf_ref, ubuf_next_ref, x_ref, bc_ref,
               convw_ref, convb_ref, dtb_ref, a, dskip_ref, normw_ref, shift_ref,
               headsel_ref, headdiag_ref, chansel_ref, state_ref, causal, head_of_lane, *, n_heads):
    L, P, N, G = SSM_CHUNK, SSM_HEAD_DIM, SSM_STATE, SSM_GROUPS
    H = n_heads
    hpg = H // G
    gw = hpg * P
    slab = gw + 2 * N
    hist = ubuf_ref.shape[0] - L

    ubuf_ref[hist:hist + L, :] = xbc_ref[...]
    for g in range(G):
        cs = slice(g * slab, (g + 1) * slab)
        u = ubuf_ref[:, cs]
        shifted = _dot(shift_ref[...], u)
        acc = convb_ref[:, cs] + convw_ref[SSM_CONV - 1:SSM_CONV, cs] * u[hist:].astype(F32)
        for k in range(SSM_CONV - 1):
            acc = acc + convw_ref[k:k + 1, cs] * shifted[k * L:(k + 1) * L]
        act = _silu_of_twice(acc)
        x_ref[:, g * gw:(g + 1) * gw] = act[:, :gw]
        bc_ref[:, g * 2 * N:(g + 1) * 2 * N] = act[:, gw:].astype(BF16)
    ubuf_next_ref[0:hist, :] = ubuf_ref[L:L + hist, :]

    dt = _softplus(dt_raw + dtb_ref[...])
    acum = _cumsum_lanes(dt * a)
    last = acum[:, L - 1:L]
    wrow = dt * jnp.exp(last - acum)
    acum2 = acum * LOG2E

    neg_rows = [jnp.concatenate([-p] * H, axis=1) * headdiag_ref[...] for p in _split3_bf16(acum2)]
    neg_rows.append(jnp.zeros((H, H * L), BF16))
    seg_lhs = jnp.concatenate([_split3_t(acum2), jnp.ones((L, 4 * H), BF16)], axis=1)
    chan_lhs = jnp.concatenate([_split3_t(dt), _split3_t(wrow), _split3_t(jnp.exp(acum))], axis=0)

    def b_of(g):
        return bc_ref[:, g * 2 * N:g * 2 * N + N]

    def c_of(g):
        return bc_ref[:, g * 2 * N + N:(g + 1) * 2 * N]

    def expand(g):
        hs = slice(g * hpg * L, (g + 1) * hpg * L)
        seg_rhs = jnp.concatenate([headsel_ref[:, hs]] + [r[:, hs] for r in neg_rows], axis=0)
        return (_dot(seg_lhs, seg_rhs), _dot(chan_lhs, chansel_ref[:, g * gw:(g + 1) * gw]),
                _dot_nt(c_of(g), b_of(g)))

    expanded = expand(0)
    for g in range(G):
        sl = slice(g * gw, (g + 1) * gw)
        seg, chan, cb = expanded
        if g + 1 < G:
            expanded = expand(g + 1)
        x_g, b_g, c_g = x_ref[:, sl], b_of(g), c_of(g)
        st = state_ref[g]
        y_state = _dot(c_g, st.astype(BF16))
        xw = (x_g * chan[L:2 * L]).astype(BF16)
        state_ref[g] = (st * chan[3 * L - 1:3 * L]
                        + lax.dot_general(b_g, xw, TN_DIMS, preferred_element_type=F32))
        m_g = jnp.concatenate(
            [(jnp.exp2(jnp.where(causal, seg[:, r * L:(r + 1) * L], -jnp.inf)) * cb).astype(BF16)
             for r in range(hpg)], axis=1)
        xdt = (x_g * chan[0:L]).astype(BF16)
        xdt_diag = jnp.concatenate(
            [jnp.where(head_of_lane == r, xdt, jnp.zeros_like(xdt)) for r in range(hpg)], axis=0)
        y = _dot(m_g, xdt_diag) + y_state * chan[2 * L:3 * L] + dskip_ref[:, sl] * x_g
        y = y * _silu_of_twice(z_ref[:, sl].astype(F32))
        y = y * lax.rsqrt(jnp.mean(y * y, axis=-1, keepdims=True) + RMS_EPS)
        o_ref[:, sl] = (y * normw_ref[:, sl]).astype(BF16)


def _ssd_constants(n_heads, d_inner):
    L, H = SSM_CHUNK, n_heads
    hist = 2 * V7X_SUBLANES
    t = jnp.arange(L)[:, None]
    src = jnp.arange(hist + L)[None, :]
    shift = jnp.concatenate([(src == t + hist - d).astype(BF16)
                             for d in range(SSM_CONV - 1, 0, -1)], axis=0)
    k = jnp.arange(4 * H)[:, None]
    live = k < 3 * H
    headsel = (live & (k % H == jnp.arange(H * L)[None, :] // L)).astype(BF16)
    headdiag = (jnp.arange(H)[:, None] == jnp.arange(H * L)[None, :] // L).astype(BF16)
    chansel = (live & (k % H == jnp.arange(d_inner)[None, :] // SSM_HEAD_DIM)).astype(BF16)
    return shift, headsel, headdiag, chansel, hist


def _conv_group_major(t, d_inner):
    G, N = SSM_GROUPS, SSM_STATE
    lead = t.shape[:-1]
    x = t[..., :d_inner].reshape(lead + (G, d_inner // G))
    b = t[..., d_inner:d_inner + G * N].reshape(lead + (G, N))
    c = t[..., d_inner + G * N:].reshape(lead + (G, N))
    return jnp.concatenate([x, b, c], axis=-1).reshape(t.shape)


def _ssd(proj, dt_t, conv_w, conv_b, dt_bias, a_log, d_skip, norm_w, *, bsz, seq, d_inner, n_heads):
    conv_dim = conv_w.shape[1]
    L = SSM_CHUNK
    rows = SSM_STEP_CHUNKS * L
    assert seq % rows == 0 and conv_dim % d_inner == 0 and 4 * n_heads == L
    assert conv_dim == d_inner + 2 * SSM_GROUPS * SSM_STATE
    nc = seq // rows
    z_blk = conv_dim // d_inner
    gw = d_inner // SSM_GROUPS
    shift, headsel, headdiag, chansel, hist = _ssd_constants(n_heads, d_inner)
    tok = lambda b, c: (b * nc + c, 0)
    return pl.pallas_call(
        functools.partial(_ssd_kernel, n_heads=n_heads),
        grid=(bsz, nc),
        in_specs=[pl.BlockSpec((rows, conv_dim), tok),
                  pl.BlockSpec((rows, d_inner), lambda b, c: (b * nc + c, z_blk)),
                  pl.BlockSpec((1, n_heads, rows), lambda b, c: (b, 0, c)),
                  _resident((SSM_CONV, conv_dim)), _resident((1, conv_dim)),
                  _resident((n_heads, 1)), _resident((n_heads, 1)),
                  _resident((1, d_inner)), _resident((1, d_inner)),
                  _resident(shift.shape), _resident(headsel.shape), _resident(headdiag.shape),
                  _resident(chansel.shape)],
        out_specs=pl.BlockSpec((rows, d_inner), tok),
        out_shape=jax.ShapeDtypeStruct((bsz * seq, d_inner), BF16),
        scratch_shapes=[pltpu.VMEM((SSM_GROUPS, SSM_STATE, gw), F32),
                        pltpu.VMEM((SSM_STEP_CHUNKS, hist + L, conv_dim), BF16),
                        pltpu.VMEM((SSM_STEP_CHUNKS, L, d_inner), F32),
                        pltpu.VMEM((SSM_STEP_CHUNKS, L, conv_dim - d_inner), BF16)],
        compiler_params=_params(("parallel", "arbitrary")),
        name="ssd",
    )(proj, proj, dt_t, conv_w, conv_b.reshape(1, conv_dim),
      dt_bias.reshape(n_heads, 1), a_log.reshape(n_heads, 1),
      jnp.repeat(d_skip, SSM_HEAD_DIM).reshape(1, d_inner), norm_w.reshape(1, d_inner),
      shift, headsel, headdiag, chansel)


def _moba_kernel(qt_ref, k_ref, vt_ref, o_ref, *, nb):
    blk, D = MOBA_BLOCK, ATT_HEAD_DIM
    seq = nb * blk

    bi = lax.broadcasted_iota(jnp.int32, (nb, seq), 0)
    ti = lax.broadcasted_iota(jnp.int32, (nb, seq), 1)
    in_blk = (ti >= bi * blk) & (ti < (bi + 1) * blk)
    avg = jnp.where(in_blk, 1.0 / blk, 0.0).astype(BF16)

    key_idx = lax.broadcasted_iota(jnp.int32, (blk, blk), 0)
    qry_idx = lax.broadcasted_iota(jnp.int32, (blk, blk), 1)
    causal = key_idx <= qry_idx

    def scores(task):
        hh, i = task
        return _dot(k_ref[hh, 0:(i + 1) * blk, :], qt_ref[hh, :, i * blk:(i + 1) * blk])

    def gate_scores(hh):
        k_mean = _dot(avg, k_ref[hh])
        gate3 = _dot(jnp.concatenate(_split3_bf16(k_mean), axis=0), qt_ref[hh])
        return gate3[0:nb] + gate3[nb:2 * nb] + gate3[2 * nb:3 * nb]

    tasks = [(hh, i) for hh in range(MOBA_STEP_HEADS) for i in range(nb)]
    ahead = MOBA_SCORES_AHEAD
    s_queue = [scores(t) for t in tasks[:ahead]]
    gates = {}

    for n, (hh, i) in enumerate(tasks):
        qs = slice(i * blk, (i + 1) * blk)
        n_keys = (i + 1) * blk
        s_all = s_queue.pop(0)
        if n + ahead < len(tasks):
            s_queue.append(scores(tasks[n + ahead]))
        if hh not in gates:
            gates[hh] = gate_scores(hh)
        g_rows = [gates[hh][j:j + 1, qs] for j in range(i)]

        blocks = []
        for j in range(i + 1):
            s = s_all[j * blk:(j + 1) * blk]
            if j == i:
                s = jnp.where(causal, s, -jnp.inf)
            elif i > MOBA_TOPK:
                rank = jnp.zeros((1, blk), jnp.int32)
                for jj in range(i):
                    if jj < j:
                        rank = rank + (g_rows[jj] >= g_rows[j]).astype(jnp.int32)
                    elif jj > j:
                        rank = rank + (g_rows[jj] > g_rows[j]).astype(jnp.int32)
                s = jnp.where(rank < MOBA_TOPK, s, -jnp.inf)
            blocks.append(s)

        m = functools.reduce(jnp.maximum, [jnp.max(s, axis=0, keepdims=True) for s in blocks])
        p_all = jnp.concatenate([jnp.exp2(s - m).astype(BF16) for s in blocks], axis=0)
        acc = _dot(vt_ref[hh, :, 0:n_keys], p_all)
        o_ref[qs, hh * D:(hh + 1) * D] = (acc[0:D] * (1.0 / acc[D:D + 1])).T.astype(BF16)


def _moba(k_h, q_t, v_t, *, bsz, seq, n_heads):
    d, hps = ATT_HEAD_DIM, MOBA_STEP_HEADS
    assert seq % MOBA_BLOCK == 0 and n_heads % hps == 0
    nb = seq // MOBA_BLOCK
    return pl.pallas_call(
        functools.partial(_moba_kernel, nb=nb),
        grid=(bsz, n_heads // hps),
        in_specs=[pl.BlockSpec((None, hps, d, seq), lambda b, h: (b, h, 0, 0)),
                  pl.BlockSpec((None, hps, seq, d), lambda b, h: (b, h, 0, 0)),
                  pl.BlockSpec((None, hps, d + ONES_ROWS, seq), lambda b, h: (b, h, 0, 0))],
        out_specs=pl.BlockSpec((seq, hps * d), lambda b, h: (b, h)),
        out_shape=jax.ShapeDtypeStruct((bsz * seq, n_heads * d), BF16),
        compiler_params=_params(("parallel", "arbitrary")),
        name="moba",
    )(q_t, k_h, v_t)


def _merge_ln_kernel(ys_ref, ya_ref, gr_ref, h_ref, wso_ref, wao_ref, wo_ref, bg_ref,
                     lg_ref, lb_ref, o_ref):
    d = h_ref.shape[1]
    for rs in _sub_tiles(h_ref.shape[0]):
        y_ssm = _dot(ys_ref[rs, :], wso_ref[...])
        y_att = _dot(ya_ref[rs, :], wao_ref[...])
        gates = jax.nn.sigmoid(gr_ref[rs, :].astype(F32) + bg_ref[...])
        merged = gates[:, :d] * y_ssm + gates[:, d:] * y_att
        mix = _dot(merged.astype(BF16), wo_ref[...])
        o_ref[rs, :] = _layer_norm(DEEPNORM_ALPHA * h_ref[rs, :] + mix, lg_ref[...], lb_ref[...])


def _merge_ln(y_ssm, y_att, proj, h, w_ssm_out, w_att_out, w_o, b_gate, ln_g, ln_b, *, gate_col):
    n, d = h.shape
    d_inner, att_w = y_ssm.shape[1], y_att.shape[1]
    row = lambda i: (i, 0)
    return pl.pallas_call(
        _merge_ln_kernel,
        grid=(n // FFN_ROWS,),
        in_specs=[pl.BlockSpec((FFN_ROWS, d_inner), row),
                  pl.BlockSpec((FFN_ROWS, att_w), row),
                  pl.BlockSpec((FFN_ROWS, 2 * d), lambda i: (i, gate_col)),
                  pl.BlockSpec((FFN_ROWS, d), row),
                  _resident((d_inner, d)), _resident((att_w, d)), _resident((d, d)),
                  _resident((1, 2 * d)), _resident((1, d)), _resident((1, d))],
        out_specs=pl.BlockSpec((FFN_ROWS, d), row),
        out_shape=jax.ShapeDtypeStruct((n, d), F32),
        compiler_params=_params(("parallel",)),
        name="merge_ln",
    )(y_ssm, y_att, proj, h, w_ssm_out.astype(BF16), w_att_out.astype(BF16), w_o.astype(BF16),
      b_gate.reshape(1, 2 * d), ln_g.reshape(1, d), ln_b.reshape(1, d))


def kernel(x, ffn1_w_gate, ffn1_w_up, ffn1_w_down, ln1_g, ln1_b, w_in, conv_w, conv_b, dt_bias,
           a_log, d_skip, ssm_norm_w, w_ssm_out, w_att_out, b_gate, w_o, ln2_g, ln2_b,
           ffn2_w_gate, ffn2_w_up, ffn2_w_down, ln3_g, ln3_b):
    bsz, seq, d = x.shape
    n = bsz * seq
    d_inner = w_ssm_out.shape[0]
    conv_dim = conv_w.shape[1]
    n_ssm_heads = dt_bias.shape[0]
    att_w = w_att_out.shape[0]
    n_att_heads = att_w // ATT_HEAD_DIM

    edges = [0]
    for width in (d_inner, conv_dim, n_ssm_heads, att_w, att_w, att_w, 2 * d):
        edges.append(edges[-1] + width)
    assert edges[-1] == w_in.shape[1]
    w_z, w_xbc, w_dt, w_q, w_k, w_v, w_g = (w_in[:, lo:hi] for lo, hi in zip(edges[:-1], edges[1:]))
    w_q = w_q * (LOG2E / math.sqrt(ATT_HEAD_DIM))
    w_xbc = _conv_group_major(w_xbc, d_inner)
    w_main = jnp.concatenate([w_xbc, 0.5 * w_z, w_g], axis=1).astype(BF16)
    conv_w_half = _conv_group_major(0.5 * conv_w, d_inner)
    conv_b_half = _conv_group_major(0.5 * conv_b, d_inner)
    g_off = conv_dim + d_inner
    assert g_off % (2 * d) == 0

    h1, h1b = _ffn_ln(x.reshape(n, d), ffn1_w_gate, ffn1_w_up, ffn1_w_down, ln1_g, ln1_b,
                      with_bf16_out=True)
    proj, k_h, q_t, v_t, dt_t = _proj(h1b, w_main, w_k.astype(BF16), w_q.T.astype(BF16), w_v.T.astype(BF16),
                                      w_dt.T.astype(BF16), bsz, seq)
    y_ssm = _ssd(proj, dt_t, conv_w_half, conv_b_half, dt_bias, a_log, d_skip, ssm_norm_w,
                 bsz=bsz, seq=seq, d_inner=d_inner, n_heads=n_ssm_heads)
    y_att = _moba(k_h, q_t, v_t, bsz=bsz, seq=seq, n_heads=n_att_heads)
    h2 = _merge_ln(y_ssm, y_att, proj, h1, w_ssm_out, w_att_out, w_o, b_gate, ln2_g, ln2_b,
                   gate_col=g_off // (2 * d))
    (out,) = _ffn_ln(h2, ffn2_w_gate, ffn2_w_up, ffn2_w_down, ln3_g, ln3_b, with_bf16_out=False)
    return out.reshape(bsz, seq, d)
```
